```python
import math
import jax
import jax.numpy as jnp
from jax import lax
import numpy as np


D_MODEL = 4096
BATCH = 2
SEQ = 8192
DEPTH = 4

GRID_W = 64
CTX_LEN = 256
N_MIXERS = 4
HEAD_DIM = 128
A_HEADS = D_MODEL // HEAD_DIM
A_KV_HEADS = A_HEADS // 4
A_GROUP = A_HEADS // A_KV_HEADS
WINDOW = 128
BLOCK = 128
SC_WIDTH = 3
C_HEADS = D_MODEL // (2 * HEAD_DIM)
C_WIDTH = C_HEADS * 2 * HEAD_DIM
CONF_WIDTH = 31
ROPE_FREQS = HEAD_DIM // 4
ROPE_BASE = 10000.0
EPS = 1e-6
CTX_READERS = (0, 2)

kernel_name = 'hybrid_interleaved_flow_block'


def rms_norm(x, g):
    xf = x.astype(jnp.float32)
    y = xf * lax.rsqrt(jnp.mean(xf * xf, axis=-1, keepdims=True) + EPS)
    return (y * g.astype(jnp.float32)).astype(x.dtype)


def layer_norm(x, g, b):
    xf = x.astype(jnp.float32)
    xc = xf - jnp.mean(xf, axis=-1, keepdims=True)
    y = xc * lax.rsqrt(jnp.mean(xc * xc, axis=-1, keepdims=True) + EPS)
    return (y * g.astype(jnp.float32) + b.astype(jnp.float32)).astype(x.dtype)


def axial_rope_tables(n):
    rows = n // GRID_W
    r, col = jnp.meshgrid(jnp.arange(rows), jnp.arange(GRID_W), indexing='ij')
    pos = jnp.stack([r.reshape(-1), col.reshape(-1)], axis=-1).astype(jnp.float32)
    inv_freq = ROPE_BASE ** (-jnp.arange(ROPE_FREQS, dtype=jnp.float32) / ROPE_FREQS)
    ang = pos[:, :, None] * inv_freq
    return jnp.cos(ang), jnp.sin(ang)


def apply_rope(x, cos, sin):
    b, n, h, d = x.shape
    xr = x.astype(jnp.float32).reshape(b, n, h, 2, 2, ROPE_FREQS)
    x1, x2 = xr[..., 0, :], xr[..., 1, :]
    cs, sn = cos[None, :, None], sin[None, :, None]
    y = jnp.stack([x1 * cs - x2 * sn, x1 * sn + x2 * cs], axis=-2)
    return y.reshape(b, n, h, d).astype(x.dtype)


def depthwise_conv(x, w, b):
    k = w.shape[0]
    y = lax.conv_general_dilated(x, w[:, None, :], window_strides=(1,),
                                 padding=[(k // 2, k // 2)],
                                 dimension_numbers=('NWC', 'WIO', 'NWC'),
                                 feature_group_count=x.shape[-1])
    return y + b


def window_gqa_mixer(h, hc, w_in, sink, w_out, cos, sin, ctx_out):
    bsz, n, _ = h.shape
    L = hc.shape[1]
    qd, kvd = A_HEADS * HEAD_DIM, A_KV_HEADS * HEAD_DIM
    splits = [qd, qd + kvd, qd + 2 * kvd]
    scale = HEAD_DIM ** -0.5
    q, k, v, g = jnp.split(h @ w_in, splits, axis=-1)
    q = apply_rope(q.reshape(bsz, n, A_HEADS, HEAD_DIM), cos, sin)
    k = apply_rope(k.reshape(bsz, n, A_KV_HEADS, HEAD_DIM), cos, sin)
    v = v.reshape(bsz, n, A_KV_HEADS, HEAD_DIM)
    if ctx_out:
        qc, kc, vc, gc = jnp.split(hc @ w_in, splits, axis=-1)
    else:
        kc, vc = jnp.split(hc @ w_in[:, qd:qd + 2 * kvd], 2, axis=-1)
    kc = kc.reshape(bsz, L, A_KV_HEADS, HEAD_DIM)
    vc = vc.reshape(bsz, L, A_KV_HEADS, HEAD_DIM)
    sink_f = sink.astype(jnp.float32).reshape(A_KV_HEADS, A_GROUP)

    nb = n // BLOCK
    qb = q.reshape(bsz, nb, BLOCK, A_KV_HEADS, A_GROUP, HEAD_DIM)

    def band(t):
        tp = jnp.pad(t.reshape(bsz, nb, BLOCK, A_KV_HEADS, HEAD_DIM),
                     ((0, 0), (1, 1), (0, 0), (0, 0), (0, 0)))
        return jnp.concatenate([tp[:, :-2], tp[:, 1:-1], tp[:, 2:]], axis=2)

    kw, vw = band(k), band(v)
    kw_len = 3 * BLOCK
    qpos = jnp.arange(nb)[:, None] * BLOCK + jnp.arange(BLOCK)[None, :]
    kpos = (jnp.arange(nb)[:, None] - 1) * BLOCK + jnp.arange(kw_len)[None, :]
    valid = (jnp.abs(qpos[:, :, None] - kpos[:, None, :]) <= WINDOW) & \
        ((kpos >= 0) & (kpos < n))[:, None, :]
    s_lat = jnp.einsum('bnqhgd,bnkhd->bnhgqk', qb, kw).astype(jnp.float32) * scale
    s_lat = jnp.where(valid[None, :, None, None], s_lat, -jnp.inf)
    s_ctx = jnp.einsum('bnqhgd,bchd->bnhgqc', qb, kc).astype(jnp.float32) * scale
    s_sink = jnp.broadcast_to(sink_f[None, None, :, :, None, None], s_ctx.shape[:-1] + (1,))
    p = jax.nn.softmax(jnp.concatenate([s_lat, s_ctx, s_sink], axis=-1), axis=-1).astype(v.dtype)
    o = jnp.einsum('bnhgqk,bnkhd->bnqhgd', p[..., :kw_len], vw) + \
        jnp.einsum('bnhgqc,bchd->bnqhgd', p[..., kw_len:kw_len + L], vc)
    out = (o.reshape(bsz, n, qd) * jax.nn.silu(g)) @ w_out

    out_c = None
    if ctx_out:
        qc = qc.reshape(bsz, L, A_KV_HEADS, A_GROUP, HEAD_DIM)
        sc = jnp.einsum('blhgd,bchd->bhglc', qc, kc).astype(jnp.float32) * scale
        sc_sink = jnp.broadcast_to(sink_f[None, :, :, None, None], sc.shape[:-1] + (1,))
        pc = jax.nn.softmax(jnp.concatenate([sc, sc_sink], axis=-1), axis=-1).astype(vc.dtype)
        oc = jnp.einsum('bhglc,bchd->blhgd', pc[..., :L], vc)
        out_c = (oc.reshape(bsz, L, qd) * jax.nn.silu(gc)) @ w_out
    return out, out_c


def short_conv_mixer(h, hc, w_in, conv_w, conv_b, w_out, ctx_out):
    def branch(t):
        bg, cg, u, g = jnp.split(t @ w_in, 4, axis=-1)
        y = bg * depthwise_conv(cg * u, conv_w, conv_b)
        return (y * jax.nn.silu(g)) @ w_out
    out = branch(h)
    out_c = branch(hc) if ctx_out else None
    return out, out_c


def diff_core(q, k, v, lam):
    s = jnp.einsum('bqhjd,bkhjd->bhjqk', q, k).astype(jnp.float32) * (HEAD_DIM ** -0.5)
    p = jax.nn.softmax(s, axis=-1)
    a = p[:, :, 0] - lam * p[:, :, 1]
    return jnp.einsum('bhqk,bkhe->bqhe', a.astype(v.dtype), v)


def diff_attn_mixer(h, hc, w_in, lq1, lk1, lq2, lk2, head_g, w_out, cos, sin, lam_init, ctx_out):
    bsz, n, _ = h.shape
    L = hc.shape[1]
    splits = [C_WIDTH, 2 * C_WIDTH, 3 * C_WIDTH]
    f32 = jnp.float32
    lam = jnp.exp(jnp.sum(lq1.astype(f32) * lk1.astype(f32))) - \
        jnp.exp(jnp.sum(lq2.astype(f32) * lk2.astype(f32))) + lam_init
    q, k, v, g = jnp.split(h @ w_in, splits, axis=-1)
    q = apply_rope(q.reshape(bsz, n, 2 * C_HEADS, HEAD_DIM), cos, sin).reshape(bsz, n, C_HEADS, 2, HEAD_DIM)
    k = apply_rope(k.reshape(bsz, n, 2 * C_HEADS, HEAD_DIM), cos, sin).reshape(bsz, n, C_HEADS, 2, HEAD_DIM)
    v = v.reshape(bsz, n, C_HEADS, 2 * HEAD_DIM)
    if ctx_out:
        qc, kc, vc, gc = jnp.split(hc @ w_in, splits, axis=-1)
    else:
        kc, vc = jnp.split(hc @ w_in[:, C_WIDTH:3 * C_WIDTH], 2, axis=-1)
    kc = kc.reshape(bsz, L, C_HEADS, 2, HEAD_DIM)
    vc = vc.reshape(bsz, L, C_HEADS, 2 * HEAD_DIM)
    k_all = jnp.concatenate([kc, k], axis=1)
    v_all = jnp.concatenate([vc, v], axis=1)
    nb = n // BLOCK
    qb = jnp.moveaxis(q.reshape(bsz, nb, BLOCK, C_HEADS, 2, HEAD_DIM), 1, 0)
    o = lax.map(lambda qblk: diff_core(qblk, k_all, v_all, lam), qb)
    o = jnp.moveaxis(o, 0, 1).reshape(bsz, n, C_HEADS, 2 * HEAD_DIM)
    o = rms_norm(o, head_g) * (1.0 - lam_init)
    out = (o.reshape(bsz, n, C_WIDTH) * jax.nn.silu(g)) @ w_out
    out_c = None
    if ctx_out:
        oc = diff_core(qc.reshape(bsz, L, C_HEADS, 2, HEAD_DIM), kc, vc, lam)
        oc = rms_norm(oc, head_g) * (1.0 - lam_init)
        out_c = (oc.reshape(bsz, L, C_WIDTH) * jax.nn.silu(gc)) @ w_out
    return out, out_c


def conformer_conv_mixer(h, hc, w_in, conv_w, conv_b, ln_g, ln_b, w_out, ctx_out):
    def branch(t):
        a, b_, g = jnp.split(t @ w_in, 3, axis=-1)
        u = a * jax.nn.sigmoid(b_)
        u = depthwise_conv(u, conv_w, conv_b)
        u = jax.nn.silu(layer_norm(u, ln_g, ln_b))
        return (u * jax.nn.silu(g)) @ w_out
    out = branch(h)
    out_c = branch(hc) if ctx_out else None
    return out, out_c


def _count(m):
    return len(range(m, DEPTH, N_MIXERS))


def setup_inputs(seed: int = 0) -> dict:
    key = jax.random.key(seed)
    ks = iter(jax.random.split(key, 32))
    f32 = jnp.float32

    def nrm(shape, s):
        return jax.random.normal(next(ks), shape, f32) * s

    D = D_MODEL
    nA, nB, nC, nD = _count(0), _count(1), _count(2), _count(3)
    qd, kvd = A_HEADS * HEAD_DIM, A_KV_HEADS * HEAD_DIM
    return {
        'x': nrm((BATCH, SEQ, D), 1.0),
        'c': nrm((BATCH, D), 1.0),
        'ctx': nrm((BATCH, CTX_LEN, D), 1.0),
        'c_ctx': nrm((D,), 1.0),
        'norm_g': 1.0 + nrm((DEPTH, D), 0.01),
        'ada_w': nrm((DEPTH, D, 3 * D), 0.25 * D ** -0.5),
        'ada_b': nrm((DEPTH, 3 * D), 0.01),
        'final_g': 1.0 + nrm((D,), 0.01),
        'a_w_in': nrm((nA, D, 2 * qd + 2 * kvd), D ** -0.5),
        'a_sink': nrm((nA, A_HEADS), 0.5),
        'a_w_out': nrm((nA, qd, D), qd ** -0.5),
        'b_w_in': nrm((nB, D, 4 * D), D ** -0.5),
        'b_conv_w': nrm((nB, SC_WIDTH, D), SC_WIDTH ** -0.5),
        'b_conv_b': nrm((nB, D), 0.01),
        'b_w_out': nrm((nB, D, D), D ** -0.5),
        'c_w_in': nrm((nC, D, 4 * C_WIDTH), D ** -0.5),
        'c_lam_q1': nrm((nC, HEAD_DIM), 0.1),
        'c_lam_k1': nrm((nC, HEAD_DIM), 0.1),
        'c_lam_q2': nrm((nC, HEAD_DIM), 0.1),
        'c_lam_k2': nrm((nC, HEAD_DIM), 0.1),
        'c_head_g': 1.0 + nrm((nC, 2 * HEAD_DIM), 0.01),
        'c_w_out': nrm((nC, C_WIDTH, D), C_WIDTH ** -0.5),
        'd_w_in': nrm((nD, D, 3 * D), D ** -0.5),
        'd_conv_w': nrm((nD, CONF_WIDTH, D), CONF_WIDTH ** -0.5),
        'd_conv_b': nrm((nD, D), 0.01),
        'd_ln_g': 1.0 + nrm((nD, D), 0.01),
        'd_ln_b': nrm((nD, D), 0.01),
        'd_w_out': nrm((nD, D, D), D ** -0.5),
    }


def reference(x, c, ctx, c_ctx, norm_g, ada_w, ada_b, final_g,
              a_w_in, a_sink, a_w_out,
              b_w_in, b_conv_w, b_conv_b, b_w_out,
              c_w_in, c_lam_q1, c_lam_k1, c_lam_q2, c_lam_k2, c_head_g, c_w_out,
              d_w_in, d_conv_w, d_conv_b, d_ln_g, d_ln_b, d_w_out):
    n = x.shape[1]
    cos, sin = axial_rope_tables(n)
    readers = [i for i in range(DEPTH) if i % N_MIXERS in CTX_READERS]
    last_reader = max(readers) if readers else -1
    s_lat = jax.nn.silu(c)
    s_ctx = jax.nn.silu(c_ctx)
    hctx = ctx
    for i in range(DEPTH):
        m, j = i % N_MIXERS, i // N_MIXERS
        ctx_out = i < last_reader
        need_ctx = ctx_out or (m in CTX_READERS)
        shift, scale, gate = jnp.split((s_lat @ ada_w[i] + ada_b[i])[:, None, :], 3, axis=-1)
        h = rms_norm(x, norm_g[i]) * (1 + scale) + shift
        hc = None
        if need_ctx:
            shift_c, scale_c, gate_c = jnp.split(s_ctx @ ada_w[i] + ada_b[i], 3)
            hc = rms_norm(hctx, norm_g[i]) * (1 + scale_c) + shift_c
        if m == 0:
            out, out_c = window_gqa_mixer(h, hc, a_w_in[j], a_sink[j], a_w_out[j], cos, sin, ctx_out)
        elif m == 1:
            out, out_c = short_conv_mixer(h, hc, b_w_in[j], b_conv_w[j], b_conv_b[j], b_w_out[j], ctx_out)
        elif m == 2:
            lam_init = 0.8 - 0.6 * math.exp(-0.3 * i)
            out, out_c = diff_attn_mixer(h, hc, c_w_in[j], c_lam_q1[j], c_lam_k1[j], c_lam_q2[j],
                                         c_lam_k2[j], c_head_g[j], c_w_out[j], cos, sin,
                                         lam_init, ctx_out)
        else:
            out, out_c = conformer_conv_mixer(h, hc, d_w_in[j], d_conv_w[j], d_conv_b[j],
                                              d_ln_g[j], d_ln_b[j], d_w_out[j], ctx_out)
        x = x + gate * out
        if ctx_out:
            hctx = hctx + gate_c * out_c
    return rms_norm(x, final_g)
```

```python
import functools
import math

import jax
import jax.numpy as jnp
from jax import lax
from jax.experimental import pallas as pl
from jax.experimental.pallas import tpu as pltpu

F32 = jnp.float32
BF16 = jnp.bfloat16

D_MODEL = 4096
DEPTH = 4
GRID_W = 64
HEAD_DIM = 128
A_HEADS = 32
A_KV_HEADS = 8
A_GROUP = 4
BLOCK = 128
SC_WIDTH = 3
C_HEADS = 16
CONF_WIDTH = 31
ROPE_FREQS = 32
ROPE_BASE = 10000.0
EPS = 1e-6
LOG2E = math.log2(math.e)
NEG_BIG = -1e30

V7X_VMEM_BYTES = 64 * 1024 * 1024
V7X_LANES = 128
V7X_SUBLANES = 8
VMEM_CEILING = V7X_VMEM_BYTES - 8 * 1024 * 1024

MM_TM = 1024
MM_TN = 1024
OUT_TM = 512
ROW_TM = 256


def _params(est_bytes, n_grid):
    limit = int(min(VMEM_CEILING, max(32 * 1024 * 1024, est_bytes * 5 // 4)))
    return pltpu.CompilerParams(
        dimension_semantics=("arbitrary",) * n_grid, vmem_limit_bytes=limit)


def _ada_body(s_ref, w_ref, b_ref, o_ref):
    s = s_ref[...]
    s = s * jax.nn.sigmoid(s)
    o_ref[...] = jnp.dot(s.astype(BF16), w_ref[...].astype(BF16),
                         preferred_element_type=F32) + b_ref[...]


def _ada_mods(cond8, ada_w, ada_b):
    depth, d, n = ada_w.shape
    tn = 512
    return pl.pallas_call(
        _ada_body,
        grid=(depth, n // tn),
        in_specs=[
            pl.BlockSpec((V7X_SUBLANES, d), lambda l, j: (0, 0)),
            pl.BlockSpec((None, d, tn), lambda l, j: (l, 0, j)),
            pl.BlockSpec((None, 1, tn), lambda l, j: (l, 0, j)),
        ],
        out_specs=pl.BlockSpec((None, V7X_SUBLANES, tn), lambda l, j: (l, 0, j)),
        out_shape=jax.ShapeDtypeStruct((depth, V7X_SUBLANES, n), F32),
        compiler_params=_params(2 * d * tn * 4 + d * tn * 2, 2),
        name="ada_mods",
    )(cond8, ada_w, ada_b.reshape(depth, 1, n))


def _norm_mod_body(x_ref, g_ref, shift_ref, scale_ref, o_ref):
    x = x_ref[...]
    y = x * lax.rsqrt(jnp.mean(x * x, axis=-1, keepdims=True) + EPS) * g_ref[...]
    o_ref[...] = (y * (1.0 + scale_ref[...]) + shift_ref[...]).astype(o_ref.dtype)


def _norm_mod(x, g, mods, mod_row):
    bsz, s, d = x.shape
    tm = min(ROW_TM, s)
    return pl.pallas_call(
        _norm_mod_body,
        grid=(bsz, s // tm),
        in_specs=[
            pl.BlockSpec((None, tm, d), lambda b, t: (b, t, 0)),
            pl.BlockSpec((1, d), lambda b, t: (0, 0)),
            pl.BlockSpec((None, 1, d), lambda b, t: (mod_row(b), 0, 0)),
            pl.BlockSpec((None, 1, d), lambda b, t: (mod_row(b), 0, 1)),
        ],
        out_specs=pl.BlockSpec((None, tm, d), lambda b, t: (b, t, 0)),
        out_shape=jax.ShapeDtypeStruct((bsz, s, d), BF16),
        compiler_params=_params(2 * tm * d * 6 + 4 * tm * d * 4, 2),
        name="norm_mod",
    )(x, g.reshape(1, d), mods, mods)


def _final_norm_body(x_ref, g_ref, o_ref):
    x = x_ref[...]
    o_ref[...] = x * lax.rsqrt(jnp.mean(x * x, axis=-1, keepdims=True) + EPS) * g_ref[...]


def _final_norm(x, g):
    bsz, s, d = x.shape
    tm = ROW_TM
    return pl.pallas_call(
        _final_norm_body,
        grid=(bsz, s // tm),
        in_specs=[pl.BlockSpec((None, tm, d), lambda b, t: (b, t, 0)),
                  pl.BlockSpec((1, d), lambda b, t: (0, 0))],
        out_specs=pl.BlockSpec((None, tm, d), lambda b, t: (b, t, 0)),
        out_shape=jax.ShapeDtypeStruct((bsz, s, d), F32),
        compiler_params=_params(2 * tm * d * 8 + 2 * tm * d * 4, 2),
        name="final_norm",
    )(x, g.reshape(1, d))


def _rope(x, cs, sn, first_half):
    swapped = jnp.where(first_half, pltpu.roll(x, 96, 1), pltpu.roll(x, 32, 1))
    return x * cs + swapped * sn


def _proj_body(h_ref, w_ref, cs_ref, sn_ref, o_ref, *, modes, q_scale, col_axis):
    acc = jnp.dot(h_ref[...], w_ref[...], preferred_element_type=F32)
    j = pl.program_id(col_axis)
    tm, tn = acc.shape
    for lo, hi, mode in modes:
        def _store(mode=mode):
            if mode == "plain":
                o_ref[...] = acc.astype(o_ref.dtype)
            elif mode == "silu":
                o_ref[...] = (acc * jax.nn.sigmoid(acc)).astype(o_ref.dtype)
            elif mode == "scale":
                o_ref[...] = (acc * q_scale).astype(o_ref.dtype)
            else:
                cs = cs_ref[...]
                sn = sn_ref[...]
                lane = lax.broadcasted_iota(jnp.int32, (tm, HEAD_DIM), 1)
                first_half = (lane % (2 * ROPE_FREQS)) < ROPE_FREQS
                for hh in range(tn // HEAD_DIM):
                    sl = slice(hh * HEAD_DIM, (hh + 1) * HEAD_DIM)
                    y = _rope(acc[:, sl], cs, sn, first_half)
                    if mode == "rope_q":
                        y = y * q_scale
                    o_ref[:, sl] = y.astype(o_ref.dtype)
        pl.when((j >= lo) & (j < hi))(_store)


def _proj(h, w, cs, sn, modes, q_scale, *, col_tiles=None, out=None, out_rows=None,
          out_row_block=0):
    bsz, s, d = h.shape
    n = w.shape[1]
    tm = min(MM_TM, s)
    tn = MM_TN
    first, count = col_tiles if col_tiles is not None else (0, n // tn)
    out_rows = out_rows if out_rows is not None else s
    body = functools.partial(_proj_body, modes=modes, q_scale=q_scale, col_axis=2)
    in_specs = [
        pl.BlockSpec((None, tm, d), lambda b, i, j: (b, i, 0)),
        pl.BlockSpec((d, tn), lambda b, i, j: (0, j + first)),
        pl.BlockSpec((tm, HEAD_DIM), lambda b, i, j: (i, 0)),
        pl.BlockSpec((tm, HEAD_DIM), lambda b, i, j: (i, 0)),
    ]
    args = [h, w, cs, sn]
    aliases = {}
    if out is not None:
        in_specs.append(pl.BlockSpec(memory_space=pl.ANY))
        args.append(out)
        aliases = {4: 0}
        kernel_fn = lambda h_ref, w_ref, cs_ref, sn_ref, _, o_ref: body(
            h_ref, w_ref, cs_ref, sn_ref, o_ref)
    else:
        kernel_fn = body
    est = 2 * (tm * d * 2 + d * tn * 2 + tm * tn * 2) + 2 * tm * tn * 4
    return pl.pallas_call(
        kernel_fn,
        grid=(bsz, s // tm, count),
        in_specs=in_specs,
        out_specs=pl.BlockSpec((None, tm, tn),
                               lambda b, i, j: (b, i + out_row_block, j + first)),
        out_shape=jax.ShapeDtypeStruct((bsz, out_rows, n), BF16),
        input_output_aliases=aliases,
        compiler_params=_params(est, 3),
        name="in_proj",
    )(*args)


def _out_body(y_ref, w_ref, x_ref, gate_ref, o_ref):
    acc = jnp.dot(y_ref[...], w_ref[...], preferred_element_type=F32)
    o_ref[...] = x_ref[...] + gate_ref[...] * acc


def _out_proj(y, w, x, mods, mod_row):
    bsz, s, d = y.shape
    n = w.shape[1]
    tm = min(OUT_TM, s)
    tn = MM_TN
    gate_blk = 2 * (D_MODEL // tn)
    est = 2 * (tm * d * 2 + d * tn * 2 + 2 * tm * tn * 4) + tm * tn * 4
    return pl.pallas_call(
        _out_body,
        grid=(bsz, s // tm, n // tn),
        in_specs=[
            pl.BlockSpec((None, tm, d), lambda b, i, j: (b, i, 0)),
            pl.BlockSpec((d, tn), lambda b, i, j: (0, j)),
            pl.BlockSpec((None, tm, tn), lambda b, i, j: (b, i, j)),
            pl.BlockSpec((None, 1, tn), lambda b, i, j: (mod_row(b), 0, gate_blk + j)),
        ],
        out_specs=pl.BlockSpec((None, tm, tn), lambda b, i, j: (b, i, j)),
        out_shape=jax.ShapeDtypeStruct((bsz, s, n), F32),
        compiler_params=_params(est, 3),
        name="out_proj",
    )(y, w, x, mods)


def _gqa_body(sink_ref, q_ref, g_ref, *rest, band, n_blocks):
    if band:
        kp_ref, ko_ref, kn_ref, vp_ref, vo_ref, vn_ref, kc_ref, vc_ref, o_ref = rest
    else:
        kc_ref, vc_ref, o_ref = rest
    i = pl.program_id(1)
    rows = A_GROUP * BLOCK
    if band:
        n_keys = 3 * BLOCK + kc_ref.shape[0]
        r = lax.broadcasted_iota(jnp.int32, (rows, n_keys), 0) % BLOCK
        c = lax.broadcasted_iota(jnp.int32, (rows, n_keys), 1)
        in_prev = (c < BLOCK) & (c >= r) & (i > 0)
        in_next = (c >= 2 * BLOCK) & (c - 2 * BLOCK <= r) & (i < n_blocks - 1)
        valid = in_prev | ((c >= BLOCK) & (c < 2 * BLOCK)) | in_next | (c >= 3 * BLOCK)
    for h in range(A_KV_HEADS):
        hs = slice(h * HEAD_DIM, (h + 1) * HEAD_DIM)
        q4 = jnp.concatenate(
            [q_ref[:, (h * A_GROUP + g) * HEAD_DIM:(h * A_GROUP + g + 1) * HEAD_DIM]
             for g in range(A_GROUP)], axis=0)
        if band:
            kcat = jnp.concatenate([kp_ref[:, hs], ko_ref[:, hs], kn_ref[:, hs], kc_ref[:, hs]], axis=0)
            vcat = jnp.concatenate([vp_ref[:, hs], vo_ref[:, hs], vn_ref[:, hs], vc_ref[:, hs]], axis=0)
        else:
            kcat = kc_ref[:, hs]
            vcat = vc_ref[:, hs]
        s = lax.dot_general(q4, kcat, (((1,), (1,)), ((), ())), preferred_element_type=F32)
        if band:
            s = jnp.where(valid, s, NEG_BIG)
        sk = jnp.concatenate(
            [jnp.full((BLOCK, 1), sink_ref[h * A_GROUP + g] * LOG2E, F32) for g in range(A_GROUP)],
            axis=0)
        m = jnp.maximum(jnp.max(s, axis=-1, keepdims=True), sk)
        p = jnp.exp2(s - m)
        l = jnp.sum(p, axis=-1, keepdims=True) + jnp.exp2(sk - m)
        o = jnp.dot(p.astype(BF16), vcat, preferred_element_type=F32) / l
        for g in range(A_GROUP):
            cs = slice((h * A_GROUP + g) * HEAD_DIM, (h * A_GROUP + g + 1) * HEAD_DIM)
            o_ref[:, cs] = (o[g * BLOCK:(g + 1) * BLOCK] * g_ref[:, cs].astype(F32)).astype(o_ref.dtype)


def _gqa(proj, proj_c, sink, band):
    src = proj if band else proj_c
    bsz, s, _ = src.shape
    l_ctx = proj_c.shape[1]
    nb = s // BLOCK
    qd = A_HEADS * HEAD_DIM
    kvd = A_KV_HEADS * HEAD_DIM
    k_blk, v_blk = 2 * qd // kvd, 2 * qd // kvd + 1
    in_specs = [
        pl.BlockSpec(memory_space=pltpu.SMEM),
        pl.BlockSpec((None, BLOCK, qd), lambda b, i: (b, i, 0)),
        pl.BlockSpec((None, BLOCK, qd), lambda b, i: (b, i, 1)),
    ]
    args = [sink, src, src]
    if band:
        for blk in (k_blk, v_blk):
            in_specs += [
                pl.BlockSpec((None, BLOCK, kvd), lambda b, i, blk=blk: (b, jnp.maximum(i - 1, 0), blk)),
                pl.BlockSpec((None, BLOCK, kvd), lambda b, i, blk=blk: (b, i, blk)),
                pl.BlockSpec((None, BLOCK, kvd), lambda b, i, blk=blk: (b, jnp.minimum(i + 1, nb - 1), blk)),
            ]
            args += [proj, proj, proj]
    in_specs += [
        pl.BlockSpec((None, l_ctx, kvd), lambda b, i: (b, 0, k_blk)),
        pl.BlockSpec((None, l_ctx, kvd), lambda b, i: (b, 0, v_blk)),
    ]
    args += [proj_c, proj_c]
    return pl.pallas_call(
        functools.partial(_gqa_body, band=band, n_blocks=nb),
        grid=(bsz, nb),
        in_specs=in_specs,
        out_specs=pl.BlockSpec((None, BLOCK, qd), lambda b, i: (b, i, 0)),
        out_shape=jax.ShapeDtypeStruct((bsz, s, qd), BF16),
        compiler_params=_params(16 * 1024 * 1024, 2),
        name="window_gqa" if band else "ctx_gqa",
    )(*args)


SCONV_HALO = 16


def _sconv_body(bg_ref, cg_ref, u_ref, g_ref, cgp_ref, up_ref, cgn_ref, un_ref,
                w_ref, b_ref, o_ref, z_ref, *, n_tiles):
    i = pl.program_id(1)
    tm = bg_ref.shape[0]
    h8 = SCONV_HALO
    z = cg_ref[...].astype(F32) * u_ref[...].astype(F32)
    z_prev = cgp_ref[h8 - 1:h8, :].astype(F32) * up_ref[h8 - 1:h8, :].astype(F32)
    z_next = cgn_ref[0:1, :].astype(F32) * un_ref[0:1, :].astype(F32)
    z_ref[h8:h8 + tm, :] = z
    z_ref[h8 - 1:h8, :] = jnp.where(i > 0, z_prev, 0.0)
    z_ref[h8 + tm:h8 + tm + 1, :] = jnp.where(i < n_tiles - 1, z_next, 0.0)
    conv = (w_ref[0:1, :] * z_ref[h8 - 1:h8 - 1 + tm, :] + w_ref[1:2, :] * z
            + w_ref[2:3, :] * z_ref[h8 + 1:h8 + 1 + tm, :] + b_ref[...])
    o_ref[...] = (bg_ref[...].astype(F32) * conv * g_ref[...].astype(F32)).astype(o_ref.dtype)


def _short_conv(proj, conv_w, conv_b):
    bsz, s, n4 = proj.shape
    d = n4 // 4
    tm = min(512, s)
    tc = 1024
    nt = s // tm
    ncb = d // tc
    h8 = SCONV_HALO
    rb = tm // h8
    last8 = s // h8 - 1

    def main(k):
        return pl.BlockSpec((None, tm, tc), lambda b, i, c, k=k: (b, i, k * ncb + c))

    def halo_prev(k):
        return pl.BlockSpec((None, h8, tc), lambda b, i, c, k=k: (b, jnp.maximum(i * rb - 1, 0), k * ncb + c))

    def halo_next(k):
        return pl.BlockSpec((None, h8, tc), lambda b, i, c, k=k: (b, jnp.minimum((i + 1) * rb, last8), k * ncb + c))

    return pl.pallas_call(
        functools.partial(_sconv_body, n_tiles=nt),
        grid=(bsz, nt, ncb),
        in_specs=[main(0), main(1), main(2), main(3),
                  halo_prev(1), halo_prev(2), halo_next(1), halo_next(2),
                  pl.BlockSpec((SC_WIDTH, tc), lambda b, i, c: (0, c)),
                  pl.BlockSpec((1, tc), lambda b, i, c: (0, c))],
        out_specs=pl.BlockSpec((None, tm, tc), lambda b, i, c: (b, i, c)),
        out_shape=jax.ShapeDtypeStruct((bsz, s, d), BF16),
        scratch_shapes=[pltpu.VMEM((tm + 2 * h8, tc), F32)],
        compiler_params=_params(2 * 5 * tm * tc * 2 + 6 * tm * tc * 4, 3),
        name="short_conv",
    )(proj, proj, proj, proj, proj, proj, proj, proj, conv_w, conv_b.reshape(1, d))


DIFF_TQ = 512
DIFF_TK = 768


def _diff_body(q_ref, k_ref, v_ref, g_ref, hg_ref, lq1_ref, lk1_ref, lq2_ref, lk2_ref,
               o_ref, *, lam_init, tk):
    lam = (jnp.exp(jnp.sum(lq1_ref[...] * lk1_ref[...], keepdims=True))
           - jnp.exp(jnp.sum(lq2_ref[...] * lk2_ref[...], keepdims=True)) + lam_init)
    tq = q_ref.shape[0]
    n_chunks = k_ref.shape[0] // tk
    dv = 2 * HEAD_DIM
    q1 = q_ref[:, :HEAD_DIM]
    q2 = q_ref[:, HEAD_DIM:]

    def one(q, kc, vc, m, l, a):
        s = lax.dot_general(q, kc, (((1,), (1,)), ((), ())), preferred_element_type=F32)
        m_new = jnp.maximum(m, jnp.max(s, axis=-1, keepdims=True))
        alpha = jnp.exp2(m - m_new)
        p = jnp.exp2(s - m_new)
        l_new = alpha * l + jnp.sum(p, axis=-1, keepdims=True)
        a_new = alpha * a + jnp.dot(p.astype(BF16), vc, preferred_element_type=F32)
        return m_new, l_new, a_new

    def step(c, carry):
        m1, l1, a1, m2, l2, a2 = carry
        off = pl.multiple_of(c * tk, tk)
        kc = k_ref[pl.ds(off, tk), :]
        vc = v_ref[pl.ds(off, tk), :]
        m1, l1, a1 = one(q1, kc[:, :HEAD_DIM], vc, m1, l1, a1)
        m2, l2, a2 = one(q2, kc[:, HEAD_DIM:], vc, m2, l2, a2)
        return m1, l1, a1, m2, l2, a2

    init = (jnp.full((tq, 1), NEG_BIG, F32), jnp.zeros((tq, 1), F32), jnp.zeros((tq, dv), F32),
            jnp.full((tq, 1), NEG_BIG, F32), jnp.zeros((tq, 1), F32), jnp.zeros((tq, dv), F32))
    m1, l1, a1, m2, l2, a2 = lax.fori_loop(0, n_chunks, step, init)
    o = a1 / l1 - lam * (a2 / l2)
    o = o * lax.rsqrt(jnp.mean(o * o, axis=-1, keepdims=True) + EPS) * hg_ref[...]
    o = o * (1.0 - lam_init)
    o_ref[...] = (o * g_ref[...].astype(F32)).astype(o_ref.dtype)


def _diff_attn(proj, s_lat, head_g, lq1, lk1, lq2, lk2, lam_init):
    bsz, s_all, n4 = proj.shape
    cw = n4 // 4
    dv = 2 * HEAD_DIM
    nh = cw // dv
    tq = DIFF_TQ
    vec = lambda a: a.reshape(1, HEAD_DIM)
    vspec = pl.BlockSpec((1, HEAD_DIM), lambda b, h, t: (0, 0))
    est = 2 * 2 * s_all * dv * 2 + 8 * tq * dv * 4 + 6 * tq * DIFF_TK * 4
    return pl.pallas_call(
        functools.partial(_diff_body, lam_init=lam_init, tk=DIFF_TK),
        grid=(bsz, nh, s_lat // tq),
        in_specs=[
            pl.BlockSpec((None, tq, dv), lambda b, h, t: (b, t, h)),
            pl.BlockSpec((None, s_all, dv), lambda b, h, t: (b, 0, nh + h)),
            pl.BlockSpec((None, s_all, dv), lambda b, h, t: (b, 0, 2 * nh + h)),
            pl.BlockSpec((None, tq, dv), lambda b, h, t: (b, t, 3 * nh + h)),
            pl.BlockSpec((1, dv), lambda b, h, t: (0, 0)),
            vspec, vspec, vspec, vspec,
        ],
        out_specs=pl.BlockSpec((None, tq, dv), lambda b, h, t: (b, t, h)),
        out_shape=jax.ShapeDtypeStruct((bsz, s_lat, cw), BF16),
        compiler_params=_params(est, 3),
        name="diff_attn",
    )(proj, proj, proj, proj, head_g.reshape(1, dv), vec(lq1), vec(lk1), vec(lq2), vec(lk2))


CONF_HALO = 16
CONF_TM = 256
CONF_TC = 512
CONF_RB = 64


def _conf_conv_body(a_ref, b_ref, ap_ref, bp_ref, an_ref, bn_ref, w_ref, cb_ref, o_ref, u_ref,
                    *, n_tiles):
    i = pl.program_id(1)
    tm, tc = a_ref.shape
    hl = CONF_HALO
    half = CONF_WIDTH // 2

    def glu(a, b):
        return a[...].astype(F32) * jax.nn.sigmoid(b[...].astype(F32))

    u_ref[hl:hl + tm, :] = glu(a_ref, b_ref)
    u_ref[0:hl, :] = jnp.where(i > 0, glu(ap_ref, bp_ref), 0.0)
    u_ref[hl + tm:hl + tm + hl, :] = jnp.where(i < n_tiles - 1, glu(an_ref, bn_ref), 0.0)
    for cc in range(tc // V7X_LANES):
        cs = slice(cc * V7X_LANES, (cc + 1) * V7X_LANES)
        w = w_ref[:, cs]
        for rb in range(tm // CONF_RB):
            acc = jnp.broadcast_to(cb_ref[:, cs], (CONF_RB, V7X_LANES))
            for k in range(CONF_WIDTH):
                r0 = rb * CONF_RB + hl - half + k
                acc = acc + w[k:k + 1, :] * u_ref[r0:r0 + CONF_RB, cs]
            o_ref[rb * CONF_RB:(rb + 1) * CONF_RB, cs] = acc


def _conf_conv(proj, conv_w, conv_b):
    bsz, s, n3 = proj.shape
    d = n3 // 3
    tm, tc, hl = CONF_TM, CONF_TC, CONF_HALO
    nt = s // tm
    ncb = d // tc
    rb = tm // hl
    last = s // hl - 1

    def main(k):
        return pl.BlockSpec((None, tm, tc), lambda b, i, c, k=k: (b, i, k * ncb + c))

    def halo_prev(k):
        return pl.BlockSpec((None, hl, tc), lambda b, i, c, k=k: (b, jnp.maximum(i * rb - 1, 0), k * ncb + c))

    def halo_next(k):
        return pl.BlockSpec((None, hl, tc), lambda b, i, c, k=k: (b, jnp.minimum((i + 1) * rb, last), k * ncb + c))

    return pl.pallas_call(
        functools.partial(_conf_conv_body, n_tiles=nt),
        grid=(bsz, nt, ncb),
        in_specs=[main(0), main(1), halo_prev(0), halo_prev(1), halo_next(0), halo_next(1),
                  pl.BlockSpec((CONF_WIDTH, tc), lambda b, i, c: (0, c)),
                  pl.BlockSpec((1, tc), lambda b, i, c: (0, c))],
        out_specs=pl.BlockSpec((None, tm, tc), lambda b, i, c: (b, i, c)),
        out_shape=jax.ShapeDtypeStruct((bsz, s, d), F32),
        scratch_shapes=[pltpu.VMEM((tm + 2 * hl, tc), F32)],
        compiler_params=_params(8 * 1024 * 1024, 3),
        name="conf_conv",
    )(proj, proj, proj, proj, proj, proj, conv_w, conv_b.reshape(1, d))


def _ln_gate_body(u_ref, g_ref, lg_ref, lb_ref, o_ref):
    u = u_ref[...]
    uc = u - jnp.mean(u, axis=-1, keepdims=True)
    y = uc * lax.rsqrt(jnp.mean(uc * uc, axis=-1, keepdims=True) + EPS) * lg_ref[...] + lb_ref[...]
    y = y * jax.nn.sigmoid(y)
    o_ref[...] = (y * g_ref[...].astype(F32)).astype(o_ref.dtype)


def _ln_gate(u, proj, ln_g, ln_b):
    bsz, s, d = u.shape
    tm = ROW_TM
    return pl.pallas_call(
        _ln_gate_body,
        grid=(bsz, s // tm),
        in_specs=[pl.BlockSpec((None, tm, d), lambda b, t: (b, t, 0)),
                  pl.BlockSpec((None, tm, d), lambda b, t: (b, t, 2)),
                  pl.BlockSpec((1, d), lambda b, t: (0, 0)),
                  pl.BlockSpec((1, d), lambda b, t: (0, 0))],
        out_specs=pl.BlockSpec((None, tm, d), lambda b, t: (b, t, 0)),
        out_shape=jax.ShapeDtypeStruct((bsz, s, d), BF16),
        compiler_params=_params(2 * tm * d * 8 + 4 * tm * d * 4, 2),
        name="ln_gate",
    )(u, proj, ln_g.reshape(1, d), ln_b.reshape(1, d))


def _rope_tables(n):
    pos = jnp.arange(n, dtype=jnp.int32)
    rc = jnp.stack([pos // GRID_W, pos % GRID_W], axis=-1).astype(F32)
    inv_freq = ROPE_BASE ** (-jnp.arange(ROPE_FREQS, dtype=F32) / ROPE_FREQS)
    ang = rc[:, :, None] * inv_freq
    cos, sin = jnp.cos(ang), jnp.sin(ang)
    cs = jnp.concatenate([cos[:, 0], cos[:, 0], cos[:, 1], cos[:, 1]], axis=-1)
    sn = jnp.concatenate([-sin[:, 0], sin[:, 0], -sin[:, 1], sin[:, 1]], axis=-1)
    return cs, sn


def _tiles(*parts):
    out, lo = [], 0
    for width, mode in parts:
        hi = lo + width // MM_TN
        out.append((lo, hi, mode))
        lo = hi
    return tuple(out)


def kernel(x, c, ctx, c_ctx, norm_g, ada_w, ada_b, final_g, a_w_in, a_sink, a_w_out, b_w_in, b_conv_w, b_conv_b, b_w_out, c_w_in, c_lam_q1, c_lam_k1, c_lam_q2, c_lam_k2, c_head_g, c_w_out, d_w_in, d_conv_w, d_conv_b, d_ln_g, d_ln_b, d_w_out):
    bsz, s, d = x.shape
    l_ctx = ctx.shape[1]
    q_scale = HEAD_DIM ** -0.5 * LOG2E
    cs, sn = _rope_tables(s)
    lat_row = lambda b: b
    ctx_row = lambda b: 2

    cond8 = jnp.zeros((V7X_SUBLANES, d), F32).at[0:bsz].set(c).at[bsz].set(c_ctx)
    mods = _ada_mods(cond8, ada_w, ada_b).reshape(DEPTH, V7X_SUBLANES, 1, 3 * d)

    qd, kvd = A_HEADS * HEAD_DIM, A_KV_HEADS * HEAD_DIM
    hctx = ctx

    w_in = a_w_in[0]
    w_in = jnp.concatenate([w_in[:, :qd], w_in[:, qd + 2 * kvd:], w_in[:, qd:qd + 2 * kvd]],
                           axis=1).astype(BF16)
    w_out = a_w_out[0].astype(BF16)
    h = _norm_mod(x, norm_g[0], mods[0], lat_row)
    hc = _norm_mod(hctx, norm_g[0], mods[0], ctx_row)
    proj = _proj(h, w_in, cs, sn,
                 _tiles((qd, "rope_q"), (qd, "silu"), (kvd, "rope_k"), (kvd, "plain")), q_scale)
    proj_c = _proj(hc, w_in, cs, sn,
                   _tiles((qd, "scale"), (qd, "silu"), (2 * kvd, "plain")), q_scale)
    y = _gqa(proj, proj_c, a_sink[0], band=True)
    y_c = _gqa(proj, proj_c, a_sink[0], band=False)
    x = _out_proj(y, w_out, x, mods[0], lat_row)
    hctx = _out_proj(y_c, w_out, hctx, mods[0], ctx_row)

    w_in = b_w_in[0].astype(BF16)
    w_out = b_w_out[0].astype(BF16)
    modes = _tiles((3 * d, "plain"), (d, "silu"))
    h = _norm_mod(x, norm_g[1], mods[1], lat_row)
    hc = _norm_mod(hctx, norm_g[1], mods[1], ctx_row)
    y = _short_conv(_proj(h, w_in, cs, sn, modes, q_scale), b_conv_w[0], b_conv_b[0])
    y_c = _short_conv(_proj(hc, w_in, cs, sn, modes, q_scale), b_conv_w[0], b_conv_b[0])
    x = _out_proj(y, w_out, x, mods[1], lat_row)
    hctx = _out_proj(y_c, w_out, hctx, mods[1], ctx_row)

    w_in = c_w_in[0].astype(BF16)
    w_out = c_w_out[0].astype(BF16)
    cw = w_in.shape[1] // 4
    lam_init = 0.8 - 0.6 * math.exp(-0.3 * 2)
    h = _norm_mod(x, norm_g[2], mods[2], lat_row)
    hc = _norm_mod(hctx, norm_g[2], mods[2], ctx_row)
    proj = _proj(h, w_in, cs, sn,
                 _tiles((cw, "rope_q"), (cw, "rope_k"), (cw, "plain"), (cw, "silu")), q_scale,
                 out_rows=s + l_ctx)
    proj = _proj(hc, w_in, cs, sn, _tiles((4 * cw, "plain")), q_scale,
                 col_tiles=(cw // MM_TN, 2 * cw // MM_TN), out=proj, out_rows=s + l_ctx,
                 out_row_block=s // l_ctx)
    y = _diff_attn(proj, s, c_head_g[0], c_lam_q1[0], c_lam_k1[0], c_lam_q2[0], c_lam_k2[0],
                   lam_init)
    x = _out_proj(y, w_out, x, mods[2], lat_row)

    w_in = d_w_in[0].astype(BF16)
    w_out = d_w_out[0].astype(BF16)
    h = _norm_mod(x, norm_g[3], mods[3], lat_row)
    proj = _proj(h, w_in, cs, sn, _tiles((2 * d, "plain"), (d, "silu")), q_scale)
    u = _conf_conv(proj, d_conv_w[0], d_conv_b[0])
    y = _ln_gate(u, proj, d_ln_g[0], d_ln_b[0])
    x = _out_proj(y, w_out, x, mods[3], lat_row)

    return _final_norm(x, final_g)
```

```python
import functools
import math

import jax
import jax.numpy as jnp
from jax import lax
from jax.experimental import pallas as pl
from jax.experimental.pallas import tpu as pltpu

F32 = jnp.float32
BF16 = jnp.bfloat16

D_MODEL = 4096
DEPTH = 4
GRID_W = 64
HEAD_DIM = 128
A_HEADS = 32
A_KV_HEADS = 8
A_GROUP = 4
BLOCK = 128
SC_WIDTH = 3
C_HEADS = 16
CONF_WIDTH = 31
ROPE_FREQS = 32
ROPE_BASE = 10000.0
EPS = 1e-6
LOG2E = math.log2(math.e)
NEG_BIG = -1e30

V7X_VMEM_BYTES = 64 * 1024 * 1024
V7X_LANES = 128
V7X_SUBLANES = 8
VMEM_CEILING = V7X_VMEM_BYTES - 8 * 1024 * 1024

MM_TM = 1024
MM_TN = 1024
OUT_TM = 512
ROW_TM = 256


def _params(est_bytes, n_grid):
    limit = int(min(VMEM_CEILING, max(32 * 1024 * 1024, est_bytes * 5 // 4)))
    return pltpu.CompilerParams(
        dimension_semantics=("arbitrary",) * n_grid, vmem_limit_bytes=limit)


def _ada_body(s_ref, w_ref, b_ref, o_ref):
    s = s_ref[...]
    s = s * jax.nn.sigmoid(s)
    o_ref[...] = jnp.dot(s.astype(BF16), w_ref[...].astype(BF16),
                         preferred_element_type=F32) + b_ref[...]


def _ada_mods(cond8, ada_w, ada_b):
    depth, d, n = ada_w.shape
    tn = 512
    return pl.pallas_call(
        _ada_body,
        grid=(depth, n // tn),
        in_specs=[
            pl.BlockSpec((V7X_SUBLANES, d), lambda l, j: (0, 0)),
            pl.BlockSpec((None, d, tn), lambda l, j: (l, 0, j)),
            pl.BlockSpec((None, 1, tn), lambda l, j: (l, 0, j)),
        ],
        out_specs=pl.BlockSpec((None, V7X_SUBLANES, tn), lambda l, j: (l, 0, j)),
        out_shape=jax.ShapeDtypeStruct((depth, V7X_SUBLANES, n), F32),
        compiler_params=_params(2 * d * tn * 4 + d * tn * 2, 2),
        name="ada_mods",
    )(cond8, ada_w, ada_b.reshape(depth, 1, n))


def _norm_mod_body(x_ref, g_ref, shift_ref, scale_ref, o_ref):
    x = x_ref[...]
    y = x * lax.rsqrt(jnp.mean(x * x, axis=-1, keepdims=True) + EPS) * g_ref[...]
    o_ref[...] = (y * (1.0 + scale_ref[...]) + shift_ref[...]).astype(o_ref.dtype)


def _norm_mod(x, g, mods, mod_row):
    bsz, s, d = x.shape
    tm = min(ROW_TM, s)
    return pl.pallas_call(
        _norm_mod_body,
        grid=(bsz, s // tm),
        in_specs=[
            pl.BlockSpec((None, tm, d), lambda b, t: (b, t, 0)),
            pl.BlockSpec((1, d), lambda b, t: (0, 0)),
            pl.BlockSpec((None, 1, d), lambda b, t: (mod_row(b), 0, 0)),
            pl.BlockSpec((None, 1, d), lambda b, t: (mod_row(b), 0, 1)),
        ],
        out_specs=pl.BlockSpec((None, tm, d), lambda b, t: (b, t, 0)),
        out_shape=jax.ShapeDtypeStruct((bsz, s, d), BF16),
        compiler_params=_params(2 * tm * d * 6 + 4 * tm * d * 4, 2),
        name="norm_mod",
    )(x, g.reshape(1, d), mods, mods)


def _final_norm_body(x_ref, g_ref, o_ref):
    x = x_ref[...]
    o_ref[...] = x * lax.rsqrt(jnp.mean(x * x, axis=-1, keepdims=True) + EPS) * g_ref[...]


def _final_norm(x, g):
    bsz, s, d = x.shape
    tm = ROW_TM
    return pl.pallas_call(
        _final_norm_body,
        grid=(bsz, s // tm),
        in_specs=[pl.BlockSpec((None, tm, d), lambda b, t: (b, t, 0)),
                  pl.BlockSpec((1, d), lambda b, t: (0, 0))],
        out_specs=pl.BlockSpec((None, tm, d), lambda b, t: (b, t, 0)),
        out_shape=jax.ShapeDtypeStruct((bsz, s, d), F32),
        compiler_params=_params(2 * tm * d * 8 + 2 * tm * d * 4, 2),
        name="final_norm",
    )(x, g.reshape(1, d))


TAB_IDENTITY, TAB_ROPE_Q, TAB_ROPE_K, TAB_SCALE = 0, 1, 2, 3
_MODE_TABLE = {"plain": TAB_IDENTITY, "silu": TAB_IDENTITY, "rope_q": TAB_ROPE_Q,
               "rope_k": TAB_ROPE_K, "scale": TAB_SCALE}


def _proj_body(h_ref, w_ref, cs_ref, sn_ref, o_ref, *, silu_range, rotate):
    acc = jnp.dot(h_ref[...], w_ref[...], preferred_element_type=F32)
    tm, tn = acc.shape
    if rotate:
        cs = cs_ref[...]
        sn = sn_ref[...]
        lane = lax.broadcasted_iota(jnp.int32, (tm, HEAD_DIM), 1)
        first_half = (lane % (2 * ROPE_FREQS)) < ROPE_FREQS
    if silu_range is not None:
        j = pl.program_id(2)
        is_silu = (j >= silu_range[0]) & (j < silu_range[1])
    for hh in range(tn // HEAD_DIM):
        sl = slice(hh * HEAD_DIM, (hh + 1) * HEAD_DIM)
        y = acc[:, sl]
        if rotate:
            swapped = jnp.where(first_half, pltpu.roll(y, 96, 1), pltpu.roll(y, 32, 1))
            y = y * cs + swapped * sn
        if silu_range is not None:
            y = jnp.where(is_silu, y * jax.nn.sigmoid(y), y)
        o_ref[:, sl] = y.astype(o_ref.dtype)


def _proj(h, w, tabs, modes, *, col_tiles=None, out=None, out_rows=None, out_row_block=0):
    bsz, s, d = h.shape
    n = w.shape[1]
    tm = min(MM_TM, s)
    tn = MM_TN
    first, count = col_tiles if col_tiles is not None else (0, n // tn)
    out_rows = out_rows if out_rows is not None else s
    rotate = any(_MODE_TABLE[m] != TAB_IDENTITY for _, _, m in modes)
    silu = [(lo, hi) for lo, hi, m in modes if m == "silu"]
    assert len(silu) <= 1
    body = functools.partial(_proj_body, silu_range=silu[0] if silu else None, rotate=rotate)

    def table_of(j):
        kind = jnp.int32(TAB_IDENTITY)
        for lo, hi, m in modes:
            if _MODE_TABLE[m] != TAB_IDENTITY:
                kind = jnp.where((j >= lo) & (j < hi), _MODE_TABLE[m], kind)
        return kind

    in_specs = [
        pl.BlockSpec((None, tm, d), lambda b, i, j: (b, i, 0)),
        pl.BlockSpec((d, tn), lambda b, i, j: (0, j + first)),
        pl.BlockSpec((None, tm, HEAD_DIM), lambda b, i, j: (table_of(j), i, 0)),
        pl.BlockSpec((None, tm, HEAD_DIM), lambda b, i, j: (table_of(j), i, 0)),
    ]
    args = [h, w, tabs[0], tabs[1]]
    aliases = {}
    if out is not None:
        in_specs.append(pl.BlockSpec(memory_space=pl.ANY))
        args.append(out)
        aliases = {4: 0}
        kernel_fn = lambda h_ref, w_ref, cs_ref, sn_ref, _, o_ref: body(
            h_ref, w_ref, cs_ref, sn_ref, o_ref)
    else:
        kernel_fn = body
    est = 2 * (tm * d * 2 + d * tn * 2 + tm * tn * 2) + 2 * tm * tn * 4
    return pl.pallas_call(
        kernel_fn,
        grid=(bsz, s // tm, count),
        in_specs=in_specs,
        out_specs=pl.BlockSpec((None, tm, tn),
                               lambda b, i, j: (b, i + out_row_block, j + first)),
        out_shape=jax.ShapeDtypeStruct((bsz, out_rows, n), BF16),
        input_output_aliases=aliases,
        compiler_params=_params(est, 3),
        name="in_proj",
    )(*args)


def _out_body(y_ref, w_ref, x_ref, gate_ref, o_ref):
    acc = jnp.dot(y_ref[...], w_ref[...], preferred_element_type=F32)
    o_ref[...] = x_ref[...] + gate_ref[...] * acc


def _out_proj(y, w, x, mods, mod_row):
    bsz, s, d = y.shape
    n = w.shape[1]
    tm = min(OUT_TM, s)
    tn = MM_TN
    gate_blk = 2 * (D_MODEL // tn)
    est = 2 * (tm * d * 2 + d * tn * 2 + 2 * tm * tn * 4) + tm * tn * 4
    return pl.pallas_call(
        _out_body,
        grid=(bsz, s // tm, n // tn),
        in_specs=[
            pl.BlockSpec((None, tm, d), lambda b, i, j: (b, i, 0)),
            pl.BlockSpec((d, tn), lambda b, i, j: (0, j)),
            pl.BlockSpec((None, tm, tn), lambda b, i, j: (b, i, j)),
            pl.BlockSpec((None, 1, tn), lambda b, i, j: (mod_row(b), 0, gate_blk + j)),
        ],
        out_specs=pl.BlockSpec((None, tm, tn), lambda b, i, j: (b, i, j)),
        out_shape=jax.ShapeDtypeStruct((bsz, s, n), F32),
        compiler_params=_params(est, 3),
        name="out_proj",
    )(y, w, x, mods)


GQA_RB = 16


def _gqa_body(sink_ref, q_ref, g_ref, *rest, band, n_blocks):
    if band:
        (kp_ref, ko_ref, kn_ref, vp_ref, vo_ref, vn_ref, kc_ref, vc_ref, o_ref,
         s_ref, p_ref, il_ref, bias_ref) = rest
    else:
        kc_ref, vc_ref, o_ref, s_ref, p_ref, il_ref = rest
    i = pl.program_id(1)
    rows_all = A_GROUP * BLOCK
    nt_dims = (((1,), (1,)), ((), ()))
    if band:
        r = lax.broadcasted_iota(jnp.int32, (BLOCK, BLOCK), 0)
        c = lax.broadcasted_iota(jnp.int32, (BLOCK, BLOCK), 1)
        bias_ref[0] = jnp.where((c >= r) & (i > 0), 0.0, NEG_BIG)
        bias_ref[1] = jnp.where((c <= r) & (i < n_blocks - 1), 0.0, NEG_BIG)
    for h in range(A_KV_HEADS):
        hs = slice(h * HEAD_DIM, (h + 1) * HEAD_DIM)
        q4 = jnp.concatenate(
            [q_ref[:, (h * A_GROUP + g) * HEAD_DIM:(h * A_GROUP + g + 1) * HEAD_DIM]
             for g in range(A_GROUP)], axis=0)
        if band:
            kcat = jnp.concatenate([kp_ref[:, hs], ko_ref[:, hs], kn_ref[:, hs], kc_ref[:, hs]], axis=0)
            vcat = jnp.concatenate([vp_ref[:, hs], vo_ref[:, hs], vn_ref[:, hs], vc_ref[:, hs]], axis=0)
        else:
            kcat = kc_ref[:, hs]
            vcat = vc_ref[:, hs]
        s_ref[...] = lax.dot_general(q4, kcat, nt_dims, preferred_element_type=F32)
        for rb in range(rows_all // GQA_RB):
            rows = slice(rb * GQA_RB, (rb + 1) * GQA_RB)
            g = rb * GQA_RB // BLOCK
            s = s_ref[rows, :]
            if band:
                qr = slice(rb * GQA_RB % BLOCK, rb * GQA_RB % BLOCK + GQA_RB)
                s = jnp.concatenate(
                    [s[:, :BLOCK] + bias_ref[0, qr, :], s[:, BLOCK:2 * BLOCK],
                     s[:, 2 * BLOCK:3 * BLOCK] + bias_ref[1, qr, :], s[:, 3 * BLOCK:]], axis=1)
            sk = jnp.full((GQA_RB, 1), sink_ref[h * A_GROUP + g] * LOG2E, F32)
            m = jnp.maximum(jnp.max(s, axis=-1, keepdims=True), sk)
            p = jnp.exp2(s - m)
            l = jnp.sum(p, axis=-1, keepdims=True) + jnp.exp2(sk - m)
            il_ref[rows, :] = jnp.broadcast_to(1.0 / l, (GQA_RB, HEAD_DIM))
            p_ref[rows, :] = p.astype(BF16)
        o = jnp.dot(p_ref[...], vcat, preferred_element_type=F32) * il_ref[...]
        for g in range(A_GROUP):
            cs = slice((h * A_GROUP + g) * HEAD_DIM, (h * A_GROUP + g + 1) * HEAD_DIM)
            o_ref[:, cs] = (o[g * BLOCK:(g + 1) * BLOCK] * g_ref[:, cs].astype(F32)).astype(o_ref.dtype)


def _gqa(proj, proj_c, sink, band):
    src = proj if band else proj_c
    bsz, s, _ = src.shape
    l_ctx = proj_c.shape[1]
    nb = s // BLOCK
    qd = A_HEADS * HEAD_DIM
    kvd = A_KV_HEADS * HEAD_DIM
    k_blk, v_blk = 2 * qd // kvd, 2 * qd // kvd + 1
    in_specs = [
        pl.BlockSpec(memory_space=pltpu.SMEM),
        pl.BlockSpec((None, BLOCK, qd), lambda b, i: (b, i, 0)),
        pl.BlockSpec((None, BLOCK, qd), lambda b, i: (b, i, 1)),
    ]
    args = [sink, src, src]
    if band:
        for blk in (k_blk, v_blk):
            in_specs += [
                pl.BlockSpec((None, BLOCK, kvd), lambda b, i, blk=blk: (b, jnp.maximum(i - 1, 0), blk)),
                pl.BlockSpec((None, BLOCK, kvd), lambda b, i, blk=blk: (b, i, blk)),
                pl.BlockSpec((None, BLOCK, kvd), lambda b, i, blk=blk: (b, jnp.minimum(i + 1, nb - 1), blk)),
            ]
            args += [proj, proj, proj]
    in_specs += [
        pl.BlockSpec((None, l_ctx, kvd), lambda b, i: (b, 0, k_blk)),
        pl.BlockSpec((None, l_ctx, kvd), lambda b, i: (b, 0, v_blk)),
    ]
    args += [proj_c, proj_c]
    n_keys = (3 * BLOCK if band else 0) + l_ctx
    rows_all = A_GROUP * BLOCK
    scratch = [pltpu.VMEM((rows_all, n_keys), F32),
               pltpu.VMEM((rows_all, n_keys), BF16),
               pltpu.VMEM((rows_all, HEAD_DIM), F32)]
    if band:
        scratch.append(pltpu.VMEM((2, BLOCK, BLOCK), F32))
    return pl.pallas_call(
        functools.partial(_gqa_body, band=band, n_blocks=nb),
        grid=(bsz, nb),
        in_specs=in_specs,
        out_specs=pl.BlockSpec((None, BLOCK, qd), lambda b, i: (b, i, 0)),
        out_shape=jax.ShapeDtypeStruct((bsz, s, qd), BF16),
        scratch_shapes=scratch,
        compiler_params=_params(16 * 1024 * 1024, 2),
        name="window_gqa" if band else "ctx_gqa",
    )(*args)


SCONV_HALO = 16


def _sconv_body(bg_ref, cg_ref, u_ref, g_ref, cgp_ref, up_ref, cgn_ref, un_ref,
                w_ref, b_ref, o_ref, z_ref, *, n_tiles):
    i = pl.program_id(1)
    tm = bg_ref.shape[0]
    h8 = SCONV_HALO
    z = cg_ref[...].astype(F32) * u_ref[...].astype(F32)
    z_prev = cgp_ref[h8 - 1:h8, :].astype(F32) * up_ref[h8 - 1:h8, :].astype(F32)
    z_next = cgn_ref[0:1, :].astype(F32) * un_ref[0:1, :].astype(F32)
    z_ref[h8:h8 + tm, :] = z
    z_ref[h8 - 1:h8, :] = jnp.where(i > 0, z_prev, 0.0)
    z_ref[h8 + tm:h8 + tm + 1, :] = jnp.where(i < n_tiles - 1, z_next, 0.0)
    conv = (w_ref[0:1, :] * z_ref[h8 - 1:h8 - 1 + tm, :] + w_ref[1:2, :] * z
            + w_ref[2:3, :] * z_ref[h8 + 1:h8 + 1 + tm, :] + b_ref[...])
    o_ref[...] = (bg_ref[...].astype(F32) * conv * g_ref[...].astype(F32)).astype(o_ref.dtype)


def _short_conv(proj, conv_w, conv_b):
    bsz, s, n4 = proj.shape
    d = n4 // 4
    tm = min(512, s)
    tc = 1024
    nt = s // tm
    ncb = d // tc
    h8 = SCONV_HALO
    rb = tm // h8
    last8 = s // h8 - 1

    def main(k):
        return pl.BlockSpec((None, tm, tc), lambda b, i, c, k=k: (b, i, k * ncb + c))

    def halo_prev(k):
        return pl.BlockSpec((None, h8, tc), lambda b, i, c, k=k: (b, jnp.maximum(i * rb - 1, 0), k * ncb + c))

    def halo_next(k):
        return pl.BlockSpec((None, h8, tc), lambda b, i, c, k=k: (b, jnp.minimum((i + 1) * rb, last8), k * ncb + c))

    return pl.pallas_call(
        functools.partial(_sconv_body, n_tiles=nt),
        grid=(bsz, nt, ncb),
        in_specs=[main(0), main(1), main(2), main(3),
                  halo_prev(1), halo_prev(2), halo_next(1), halo_next(2),
                  pl.BlockSpec((SC_WIDTH, tc), lambda b, i, c: (0, c)),
                  pl.BlockSpec((1, tc), lambda b, i, c: (0, c))],
        out_specs=pl.BlockSpec((None, tm, tc), lambda b, i, c: (b, i, c)),
        out_shape=jax.ShapeDtypeStruct((bsz, s, d), BF16),
        scratch_shapes=[pltpu.VMEM((tm + 2 * h8, tc), F32)],
        compiler_params=_params(2 * 5 * tm * tc * 2 + 6 * tm * tc * 4, 3),
        name="short_conv",
    )(proj, proj, proj, proj, proj, proj, proj, proj, conv_w, conv_b.reshape(1, d))


DIFF_TQ = 512
DIFF_TK = 768
DIFF_RB = 16


def _diff_body(q_ref, k_ref, v_ref, g_ref, hg_ref, lq1_ref, lk1_ref, lq2_ref, lk2_ref,
               o_ref, s0_ref, s1_ref, p0_ref, p1_ref, m_ref, l_ref, al_ref, acc_ref,
               *, lam_init, tk):
    tq = q_ref.shape[0]
    n_chunks = k_ref.shape[0] // tk
    n_lane_groups = tk // V7X_LANES
    nt_dims = (((1,), (1,)), ((), ()))
    s_refs = (s0_ref, s1_ref)
    p_refs = (p0_ref, p1_ref)
    m_ref[...] = jnp.full(m_ref.shape, NEG_BIG, F32)
    l_ref[...] = jnp.zeros(l_ref.shape, F32)
    acc_ref[...] = jnp.zeros(acc_ref.shape, F32)

    def scores(j, c):
        off = pl.multiple_of(c * tk, tk)
        hs = slice(j * HEAD_DIM, (j + 1) * HEAD_DIM)
        s_refs[j][...] = lax.dot_general(q_ref[:, hs], k_ref[pl.ds(off, tk), hs], nt_dims,
                                         preferred_element_type=F32)

    def softmax(j):
        for rb in range(tq // DIFF_RB):
            rows = slice(rb * DIFF_RB, (rb + 1) * DIFF_RB)
            s = s_refs[j][rows, :]
            m_old = m_ref[j, rows, :]
            m_new = jnp.maximum(m_old, jnp.max(s, axis=-1, keepdims=True))
            alpha = jnp.exp2(m_old - m_new)
            p = jnp.exp2(s - jnp.concatenate([m_new] * n_lane_groups, axis=1))
            l_ref[j, rows, :] = alpha * l_ref[j, rows, :] + jnp.sum(p, axis=-1, keepdims=True)
            m_ref[j, rows, :] = m_new
            al_ref[j, rows, :] = alpha
            p_refs[j][rows, :] = p.astype(BF16)

    def weighted_values(j, c):
        off = pl.multiple_of(c * tk, tk)
        al = al_ref[j]
        acc_ref[j] = (jnp.concatenate([al, al], axis=1) * acc_ref[j]
                      + jnp.dot(p_refs[j][...], v_ref[pl.ds(off, tk), :],
                                preferred_element_type=F32))

    def step(c, carry):
        scores(0, c)
        softmax(0)
        scores(1, c)
        weighted_values(0, c)
        softmax(1)
        weighted_values(1, c)
        return carry

    lax.fori_loop(0, n_chunks, step, 0)
    lam = (jnp.exp(jnp.sum(lq1_ref[...] * lk1_ref[...], keepdims=True))
           - jnp.exp(jnp.sum(lq2_ref[...] * lk2_ref[...], keepdims=True)) + lam_init)
    l1 = jnp.concatenate([l_ref[0], l_ref[0]], axis=1)
    l2 = jnp.concatenate([l_ref[1], l_ref[1]], axis=1)
    o = acc_ref[0] / l1 - lam * (acc_ref[1] / l2)
    o = o * lax.rsqrt(jnp.mean(o * o, axis=-1, keepdims=True) + EPS) * hg_ref[...]
    o = o * (1.0 - lam_init)
    o_ref[...] = (o * g_ref[...].astype(F32)).astype(o_ref.dtype)


def _diff_attn(proj, s_lat, head_g, lq1, lk1, lq2, lk2, lam_init):
    bsz, s_all, n4 = proj.shape
    cw = n4 // 4
    dv = 2 * HEAD_DIM
    nh = cw // dv
    tq = DIFF_TQ
    vec = lambda a: a.reshape(1, HEAD_DIM)
    vspec = pl.BlockSpec((1, HEAD_DIM), lambda b, h, t: (0, 0))
    est = 2 * 2 * s_all * dv * 2 + 8 * tq * dv * 4 + 6 * tq * DIFF_TK * 4
    return pl.pallas_call(
        functools.partial(_diff_body, lam_init=lam_init, tk=DIFF_TK),
        grid=(bsz, nh, s_lat // tq),
        in_specs=[
            pl.BlockSpec((None, tq, dv), lambda b, h, t: (b, t, h)),
            pl.BlockSpec((None, s_all, dv), lambda b, h, t: (b, 0, nh + h)),
            pl.BlockSpec((None, s_all, dv), lambda b, h, t: (b, 0, 2 * nh + h)),
            pl.BlockSpec((None, tq, dv), lambda b, h, t: (b, t, 3 * nh + h)),
            pl.BlockSpec((1, dv), lambda b, h, t: (0, 0)),
            vspec, vspec, vspec, vspec,
        ],
        out_specs=pl.BlockSpec((None, tq, dv), lambda b, h, t: (b, t, h)),
        out_shape=jax.ShapeDtypeStruct((bsz, s_lat, cw), BF16),
        scratch_shapes=[
            pltpu.VMEM((tq, DIFF_TK), F32),
            pltpu.VMEM((tq, DIFF_TK), F32),
            pltpu.VMEM((tq, DIFF_TK), BF16),
            pltpu.VMEM((tq, DIFF_TK), BF16),
            pltpu.VMEM((2, tq, V7X_LANES), F32),
            pltpu.VMEM((2, tq, V7X_LANES), F32),
            pltpu.VMEM((2, tq, V7X_LANES), F32),
            pltpu.VMEM((2, tq, dv), F32),
        ],
        compiler_params=_params(est, 3),
        name="diff_attn",
    )(proj, proj, proj, proj, head_g.reshape(1, dv), vec(lq1), vec(lk1), vec(lq2), vec(lk2))


CONF_HALO = 16
CONF_TM = 512
CONF_TC = 512
CONF_RB = 64
CONF_CHAINS = 4


def _conf_conv_body(a_ref, b_ref, ap_ref, bp_ref, an_ref, bn_ref, w_ref, cb_ref, o_ref,
                    u_ref, us_ref, *, n_tiles):
    i = pl.program_id(1)
    tm, tc = a_ref.shape
    hl = CONF_HALO
    half = CONF_WIDTH // 2
    sub = V7X_SUBLANES

    def glu(a, b):
        return a.astype(F32) * jax.nn.sigmoid(b.astype(F32))

    for cc in range(tc // V7X_LANES):
        cs = slice(cc * V7X_LANES, (cc + 1) * V7X_LANES)
        u_ref[cc, hl:hl + tm, :] = glu(a_ref[:, cs], b_ref[:, cs])
        u_ref[cc, 0:hl, :] = jnp.where(i > 0, glu(ap_ref[:, cs], bp_ref[:, cs]), 0.0)
        u_ref[cc, hl + tm:hl + tm + hl, :] = jnp.where(
            i < n_tiles - 1, glu(an_ref[:, cs], bn_ref[:, cs]), 0.0)
    span = us_ref.shape[2]
    for r in range(1, sub):
        us_ref[r - 1] = u_ref[:, r:r + span, :]

    def row_block(rb, carry):
        base = pl.multiple_of(rb * CONF_RB, CONF_RB)
        for cc in range(tc // V7X_LANES):
            cs = slice(cc * V7X_LANES, (cc + 1) * V7X_LANES)
            parts = [None] * CONF_CHAINS
            for k in range(CONF_WIDTH):
                off = hl - half + k
                r, al = off % sub, off - off % sub
                rows = pl.ds(base + al, CONF_RB)
                win = u_ref[cc, rows, :] if r == 0 else us_ref[r - 1, cc, rows, :]
                term = w_ref[k:k + 1, cs] * win
                c = k % CONF_CHAINS
                parts[c] = term if parts[c] is None else parts[c] + term
            while len(parts) > 1:
                parts = [parts[n] + parts[n + 1] for n in range(0, len(parts), 2)]
            o_ref[pl.ds(base, CONF_RB), cs] = parts[0] + cb_ref[:, cs]
        return carry

    lax.fori_loop(0, tm // CONF_RB, row_block, 0)


def _conf_conv(proj, conv_w, conv_b):
    bsz, s, n3 = proj.shape
    d = n3 // 3
    tm, tc, hl = CONF_TM, CONF_TC, CONF_HALO
    nt = s // tm
    ncb = d // tc
    rb = tm // hl
    last = s // hl - 1

    def main(k):
        return pl.BlockSpec((None, tm, tc), lambda b, i, c, k=k: (b, i, k * ncb + c))

    def halo_prev(k):
        return pl.BlockSpec((None, hl, tc), lambda b, i, c, k=k: (b, jnp.maximum(i * rb - 1, 0), k * ncb + c))

    def halo_next(k):
        return pl.BlockSpec((None, hl, tc), lambda b, i, c, k=k: (b, jnp.minimum((i + 1) * rb, last), k * ncb + c))

    return pl.pallas_call(
        functools.partial(_conf_conv_body, n_tiles=nt),
        grid=(bsz, nt, ncb),
        in_specs=[main(0), main(1), halo_prev(0), halo_prev(1), halo_next(0), halo_next(1),
                  pl.BlockSpec((CONF_WIDTH, tc), lambda b, i, c: (0, c)),
                  pl.BlockSpec((1, tc), lambda b, i, c: (0, c))],
        out_specs=pl.BlockSpec((None, tm, tc), lambda b, i, c: (b, i, c)),
        out_shape=jax.ShapeDtypeStruct((bsz, s, d), F32),
        scratch_shapes=[
            pltpu.VMEM((tc // V7X_LANES, tm + 2 * hl, V7X_LANES), F32),
            pltpu.VMEM((V7X_SUBLANES - 1, tc // V7X_LANES, tm + 2 * hl - V7X_SUBLANES, V7X_LANES),
                       F32)],
        compiler_params=_params(8 * (tm + 2 * hl) * tc * 4 + 4 * tm * tc * 2 + 2 * tm * tc * 4, 3),
        name="conf_conv",
    )(proj, proj, proj, proj, proj, proj, conv_w, conv_b.reshape(1, d))


def _ln_gate_body(u_ref, g_ref, lg_ref, lb_ref, o_ref):
    u = u_ref[...]
    uc = u - jnp.mean(u, axis=-1, keepdims=True)
    y = uc * lax.rsqrt(jnp.mean(uc * uc, axis=-1, keepdims=True) + EPS) * lg_ref[...] + lb_ref[...]
    y = y * jax.nn.sigmoid(y)
    o_ref[...] = (y * g_ref[...].astype(F32)).astype(o_ref.dtype)


def _ln_gate(u, proj, ln_g, ln_b):
    bsz, s, d = u.shape
    tm = ROW_TM
    return pl.pallas_call(
        _ln_gate_body,
        grid=(bsz, s // tm),
        in_specs=[pl.BlockSpec((None, tm, d), lambda b, t: (b, t, 0)),
                  pl.BlockSpec((None, tm, d), lambda b, t: (b, t, 2)),
                  pl.BlockSpec((1, d), lambda b, t: (0, 0)),
                  pl.BlockSpec((1, d), lambda b, t: (0, 0))],
        out_specs=pl.BlockSpec((None, tm, d), lambda b, t: (b, t, 0)),
        out_shape=jax.ShapeDtypeStruct((bsz, s, d), BF16),
        compiler_params=_params(2 * tm * d * 8 + 4 * tm * d * 4, 2),
        name="ln_gate",
    )(u, proj, ln_g.reshape(1, d), ln_b.reshape(1, d))


def _rope_tables(n, q_scale):
    pos = jnp.arange(n, dtype=jnp.int32)
    rc = jnp.stack([pos // GRID_W, pos % GRID_W], axis=-1).astype(F32)
    inv_freq = ROPE_BASE ** (-jnp.arange(ROPE_FREQS, dtype=F32) / ROPE_FREQS)
    ang = rc[:, :, None] * inv_freq
    cos, sin = jnp.cos(ang), jnp.sin(ang)
    cs = jnp.concatenate([cos[:, 0], cos[:, 0], cos[:, 1], cos[:, 1]], axis=-1)
    sn = jnp.concatenate([-sin[:, 0], sin[:, 0], -sin[:, 1], sin[:, 1]], axis=-1)
    one, zero = jnp.ones_like(cs), jnp.zeros_like(sn)
    return (jnp.stack([one, cs * q_scale, cs, one * q_scale]),
            jnp.stack([zero, sn * q_scale, sn, zero]))


def _tiles(*parts):
    out, lo = [], 0
    for width, mode in parts:
        hi = lo + width // MM_TN
        out.append((lo, hi, mode))
        lo = hi
    return tuple(out)


def kernel(x, c, ctx, c_ctx, norm_g, ada_w, ada_b, final_g, a_w_in, a_sink, a_w_out, b_w_in, b_conv_w, b_conv_b, b_w_out, c_w_in, c_lam_q1, c_lam_k1, c_lam_q2, c_lam_k2, c_head_g, c_w_out, d_w_in, d_conv_w, d_conv_b, d_ln_g, d_ln_b, d_w_out):
    bsz, s, d = x.shape
    l_ctx = ctx.shape[1]
    q_scale = HEAD_DIM ** -0.5 * LOG2E
    tabs = _rope_tables(s, q_scale)
    tabs_c = (tabs[0][:, :l_ctx], tabs[1][:, :l_ctx])
    lat_row = lambda b: b
    ctx_row = lambda b: 2

    cond8 = jnp.zeros((V7X_SUBLANES, d), F32).at[0:bsz].set(c).at[bsz].set(c_ctx)
    mods = _ada_mods(cond8, ada_w, ada_b).reshape(DEPTH, V7X_SUBLANES, 1, 3 * d)

    qd, kvd = A_HEADS * HEAD_DIM, A_KV_HEADS * HEAD_DIM
    hctx = ctx

    w_in = a_w_in[0]
    w_in = jnp.concatenate([w_in[:, :qd], w_in[:, qd + 2 * kvd:], w_in[:, qd:qd + 2 * kvd]],
                           axis=1).astype(BF16)
    w_out = a_w_out[0].astype(BF16)
    h = _norm_mod(x, norm_g[0], mods[0], lat_row)
    hc = _norm_mod(hctx, norm_g[0], mods[0], ctx_row)
    proj = _proj(h, w_in, tabs,
                 _tiles((qd, "rope_q"), (qd, "silu"), (kvd, "rope_k"), (kvd, "plain")))
    proj_c = _proj(hc, w_in, tabs_c,
                   _tiles((qd, "scale"), (qd, "silu"), (2 * kvd, "plain")))
    y = _gqa(proj, proj_c, a_sink[0], band=True)
    y_c = _gqa(proj, proj_c, a_sink[0], band=False)
    x = _out_proj(y, w_out, x, mods[0], lat_row)
    hctx = _out_proj(y_c, w_out, hctx, mods[0], ctx_row)

    w_in = b_w_in[0].astype(BF16)
    w_out = b_w_out[0].astype(BF16)
    modes = _tiles((3 * d, "plain"), (d, "silu"))
    h = _norm_mod(x, norm_g[1], mods[1], lat_row)
    hc = _norm_mod(hctx, norm_g[1], mods[1], ctx_row)
    y = _short_conv(_proj(h, w_in, tabs, modes), b_conv_w[0], b_conv_b[0])
    y_c = _short_conv(_proj(hc, w_in, tabs_c, modes), b_conv_w[0], b_conv_b[0])
    x = _out_proj(y, w_out, x, mods[1], lat_row)
    hctx = _out_proj(y_c, w_out, hctx, mods[1], ctx_row)

    w_in = c_w_in[0].astype(BF16)
    w_out = c_w_out[0].astype(BF16)
    cw = w_in.shape[1] // 4
    lam_init = 0.8 - 0.6 * math.exp(-0.3 * 2)
    h = _norm_mod(x, norm_g[2], mods[2], lat_row)
    hc = _norm_mod(hctx, norm_g[2], mods[2], ctx_row)
    proj = _proj(h, w_in, tabs,
                 _tiles((cw, "rope_q"), (cw, "rope_k"), (cw, "plain"), (cw, "silu")),
                 out_rows=s + l_ctx)
    proj = _proj(hc, w_in, tabs_c, _tiles((4 * cw, "plain")),
                 col_tiles=(cw // MM_TN, 2 * cw // MM_TN), out=proj, out_rows=s + l_ctx,
                 out_row_block=s // l_ctx)
    y = _diff_attn(proj, s, c_head_g[0], c_lam_q1[0], c_lam_k1[0], c_lam_q2[0], c_lam_k2[0],
                   lam_init)
    x = _out_proj(y, w_out, x, mods[2], lat_row)

    w_in = d_w_in[0].astype(BF16)
    w_out = d_w_out[0].astype(BF16)
    h = _norm_mod(x, norm_g[3], mods[3], lat_row)
    proj = _proj(h, w_in, tabs, _tiles((2 * d, "plain"), (d, "silu")))
    u = _conf_conv(proj, d_conv_w[0], d_conv_b[0])
    y = _ln_gate(u, proj, d_ln_g[0], d_ln_b[0])
    x = _out_proj(y, w_out, x, mods[3], lat_row)

    return _final_norm(x, final_g)
```

```python
import functools
import math

import jax
import jax.numpy as jnp
from jax import lax
from jax.experimental import pallas as pl
from jax.experimental.pallas import tpu as pltpu

F32 = jnp.float32
BF16 = jnp.bfloat16

D_MODEL = 4096
DEPTH = 4
GRID_W = 64
HEAD_DIM = 128
A_HEADS = 32
A_KV_HEADS = 8
A_GROUP = 4
BLOCK = 128
SC_WIDTH = 3
C_HEADS = 16
CONF_WIDTH = 31
ROPE_FREQS = 32
ROPE_BASE = 10000.0
EPS = 1e-6
LOG2E = math.log2(math.e)
NEG_BIG = -1e30

V7X_VMEM_BYTES = 64 * 1024 * 1024
V7X_LANES = 128
V7X_SUBLANES = 8
VMEM_CEILING = V7X_VMEM_BYTES - 8 * 1024 * 1024

MM_TM = 1024
MM_TN = 1024
OUT_TM = 1024
ROW_TM = 256


def _params(est_bytes, n_grid):
    limit = int(min(VMEM_CEILING, max(32 * 1024 * 1024, est_bytes * 5 // 4)))
    return pltpu.CompilerParams(
        dimension_semantics=("arbitrary",) * n_grid, vmem_limit_bytes=limit)


def _ada_body(s_ref, w_ref, b_ref, o_ref):
    s = s_ref[...]
    s = s * jax.nn.sigmoid(s)
    o_ref[...] = jnp.dot(s.astype(BF16), w_ref[...].astype(BF16),
                         preferred_element_type=F32) + b_ref[...]


def _ada_mods(cond8, ada_w, ada_b):
    depth, d, n = ada_w.shape
    tn = 512
    return pl.pallas_call(
        _ada_body,
        grid=(depth, n // tn),
        in_specs=[
            pl.BlockSpec((V7X_SUBLANES, d), lambda l, j: (0, 0)),
            pl.BlockSpec((None, d, tn), lambda l, j: (l, 0, j)),
            pl.BlockSpec((None, 1, tn), lambda l, j: (l, 0, j)),
        ],
        out_specs=pl.BlockSpec((None, V7X_SUBLANES, tn), lambda l, j: (l, 0, j)),
        out_shape=jax.ShapeDtypeStruct((depth, V7X_SUBLANES, n), F32),
        compiler_params=_params(2 * d * tn * 4 + d * tn * 2, 2),
        name="ada_mods",
    )(cond8, ada_w, ada_b.reshape(depth, 1, n))


def _norm_mod_body(x_ref, g_ref, shift_ref, scale_ref, o_ref):
    x = x_ref[...]
    y = x * lax.rsqrt(jnp.mean(x * x, axis=-1, keepdims=True) + EPS) * g_ref[...]
    o_ref[...] = (y * (1.0 + scale_ref[...]) + shift_ref[...]).astype(o_ref.dtype)


def _norm_mod(x, g, mods, mod_row):
    bsz, s, d = x.shape
    tm = min(ROW_TM, s)
    return pl.pallas_call(
        _norm_mod_body,
        grid=(bsz, s // tm),
        in_specs=[
            pl.BlockSpec((None, tm, d), lambda b, t: (b, t, 0)),
            pl.BlockSpec((1, d), lambda b, t: (0, 0)),
            pl.BlockSpec((None, 1, d), lambda b, t: (mod_row(b), 0, 0)),
            pl.BlockSpec((None, 1, d), lambda b, t: (mod_row(b), 0, 1)),
        ],
        out_specs=pl.BlockSpec((None, tm, d), lambda b, t: (b, t, 0)),
        out_shape=jax.ShapeDtypeStruct((bsz, s, d), BF16),
        compiler_params=_params(2 * tm * d * 6 + 4 * tm * d * 4, 2),
        name="norm_mod",
    )(x, g.reshape(1, d), mods, mods)


def _final_norm_body(x_ref, g_ref, o_ref):
    x = x_ref[...]
    o_ref[...] = x * lax.rsqrt(jnp.mean(x * x, axis=-1, keepdims=True) + EPS) * g_ref[...]


def _final_norm(x, g):
    bsz, s, d = x.shape
    tm = ROW_TM
    return pl.pallas_call(
        _final_norm_body,
        grid=(bsz, s // tm),
        in_specs=[pl.BlockSpec((None, tm, d), lambda b, t: (b, t, 0)),
                  pl.BlockSpec((1, d), lambda b, t: (0, 0))],
        out_specs=pl.BlockSpec((None, tm, d), lambda b, t: (b, t, 0)),
        out_shape=jax.ShapeDtypeStruct((bsz, s, d), F32),
        compiler_params=_params(2 * tm * d * 8 + 2 * tm * d * 4, 2),
        name="final_norm",
    )(x, g.reshape(1, d))


TAB_IDENTITY, TAB_ROPE_Q, TAB_ROPE_K, TAB_SCALE = 0, 1, 2, 3
_MODE_TABLE = {"plain": TAB_IDENTITY, "silu": TAB_IDENTITY, "rope_q": TAB_ROPE_Q,
               "rope_k": TAB_ROPE_K, "scale": TAB_SCALE}


def _proj_body(h_ref, w_ref, cs_ref, sn_ref, o_ref, *, silu_range, rotate):
    acc = jnp.dot(h_ref[...], w_ref[...], preferred_element_type=F32)
    tm, tn = acc.shape
    if rotate:
        cs = cs_ref[...]
        sn = sn_ref[...]
        lane = lax.broadcasted_iota(jnp.int32, (tm, HEAD_DIM), 1)
        first_half = (lane % (2 * ROPE_FREQS)) < ROPE_FREQS
    if silu_range is not None:
        j = pl.program_id(2)
        is_silu = (j >= silu_range[0]) & (j < silu_range[1])
    for hh in range(tn // HEAD_DIM):
        sl = slice(hh * HEAD_DIM, (hh + 1) * HEAD_DIM)
        y = acc[:, sl]
        if rotate:
            swapped = jnp.where(first_half, pltpu.roll(y, 96, 1), pltpu.roll(y, 32, 1))
            y = y * cs + swapped * sn
        if silu_range is not None:
            y = jnp.where(is_silu, y * jax.nn.sigmoid(y), y)
        o_ref[:, sl] = y.astype(o_ref.dtype)


def _proj(h, w, tabs, modes, *, col_tiles=None, out=None, out_rows=None, out_row_block=0):
    bsz, s, d = h.shape
    n = w.shape[1]
    tm = min(MM_TM, s)
    tn = MM_TN
    first, count = col_tiles if col_tiles is not None else (0, n // tn)
    out_rows = out_rows if out_rows is not None else s
    rotate = any(_MODE_TABLE[m] != TAB_IDENTITY for _, _, m in modes)
    silu = [(lo, hi) for lo, hi, m in modes if m == "silu"]
    assert len(silu) <= 1
    body = functools.partial(_proj_body, silu_range=silu[0] if silu else None, rotate=rotate)

    def table_of(j):
        kind = jnp.int32(TAB_IDENTITY)
        for lo, hi, m in modes:
            if _MODE_TABLE[m] != TAB_IDENTITY:
                kind = jnp.where((j >= lo) & (j < hi), _MODE_TABLE[m], kind)
        return kind

    in_specs = [
        pl.BlockSpec((None, tm, d), lambda b, i, j: (b, i, 0)),
        pl.BlockSpec((d, tn), lambda b, i, j: (0, j + first)),
        pl.BlockSpec((None, tm, HEAD_DIM), lambda b, i, j: (table_of(j), i, 0)),
        pl.BlockSpec((None, tm, HEAD_DIM), lambda b, i, j: (table_of(j), i, 0)),
    ]
    args = [h, w, tabs[0], tabs[1]]
    aliases = {}
    if out is not None:
        in_specs.append(pl.BlockSpec(memory_space=pl.ANY))
        args.append(out)
        aliases = {4: 0}
        kernel_fn = lambda h_ref, w_ref, cs_ref, sn_ref, _, o_ref: body(
            h_ref, w_ref, cs_ref, sn_ref, o_ref)
    else:
        kernel_fn = body
    est = 2 * (tm * d * 2 + d * tn * 2 + tm * tn * 2) + 2 * tm * tn * 4
    return pl.pallas_call(
        kernel_fn,
        grid=(bsz, s // tm, count),
        in_specs=in_specs,
        out_specs=pl.BlockSpec((None, tm, tn),
                               lambda b, i, j: (b, i + out_row_block, j + first)),
        out_shape=jax.ShapeDtypeStruct((bsz, out_rows, n), BF16),
        input_output_aliases=aliases,
        compiler_params=_params(est, 3),
        name="in_proj",
    )(*args)


def _out_body(y_ref, w_ref, x_ref, gate_ref, o_ref):
    acc = jnp.dot(y_ref[...], w_ref[...], preferred_element_type=F32)
    o_ref[...] = x_ref[...] + gate_ref[...] * acc


def _out_proj(y, w, x, mods, mod_row):
    bsz, s, d = y.shape
    n = w.shape[1]
    tm = min(OUT_TM, s)
    tn = MM_TN
    gate_blk = 2 * (D_MODEL // tn)
    est = 2 * (tm * d * 2 + d * tn * 2 + 2 * tm * tn * 4) + tm * tn * 4
    return pl.pallas_call(
        _out_body,
        grid=(bsz, s // tm, n // tn),
        in_specs=[
            pl.BlockSpec((None, tm, d), lambda b, i, j: (b, i, 0)),
            pl.BlockSpec((d, tn), lambda b, i, j: (0, j)),
            pl.BlockSpec((None, tm, tn), lambda b, i, j: (b, i, j)),
            pl.BlockSpec((None, 1, tn), lambda b, i, j: (mod_row(b), 0, gate_blk + j)),
        ],
        out_specs=pl.BlockSpec((None, tm, tn), lambda b, i, j: (b, i, j)),
        out_shape=jax.ShapeDtypeStruct((bsz, s, n), F32),
        compiler_params=_params(est, 3),
        name="out_proj",
    )(y, w, x, mods)


GQA_RB = 16


def _gqa_body(sink_ref, q_ref, g_ref, *rest, band, n_blocks):
    if band:
        (kp_ref, ko_ref, kn_ref, vp_ref, vo_ref, vn_ref, kc_ref, vc_ref, o_ref,
         s_ref, p_ref, il_ref, bias_ref) = rest
    else:
        kc_ref, vc_ref, o_ref, s_ref, p_ref, il_ref = rest
    i = pl.program_id(1)
    rows_all = A_GROUP * BLOCK
    nt_dims = (((1,), (1,)), ((), ()))
    if band:
        r = lax.broadcasted_iota(jnp.int32, (BLOCK, BLOCK), 0)
        c = lax.broadcasted_iota(jnp.int32, (BLOCK, BLOCK), 1)
        bias_ref[0] = jnp.where((c >= r) & (i > 0), 0.0, NEG_BIG)
        bias_ref[1] = jnp.where((c <= r) & (i < n_blocks - 1), 0.0, NEG_BIG)
    for h in range(A_KV_HEADS):
        hs = slice(h * HEAD_DIM, (h + 1) * HEAD_DIM)
        q4 = jnp.concatenate(
            [q_ref[:, (h * A_GROUP + g) * HEAD_DIM:(h * A_GROUP + g + 1) * HEAD_DIM]
             for g in range(A_GROUP)], axis=0)
        if band:
            kcat = jnp.concatenate([kp_ref[:, hs], ko_ref[:, hs], kn_ref[:, hs], kc_ref[:, hs]], axis=0)
            vcat = jnp.concatenate([vp_ref[:, hs], vo_ref[:, hs], vn_ref[:, hs], vc_ref[:, hs]], axis=0)
        else:
            kcat = kc_ref[:, hs]
            vcat = vc_ref[:, hs]
        s_ref[...] = lax.dot_general(q4, kcat, nt_dims, preferred_element_type=F32)
        for rb in range(rows_all // GQA_RB):
            rows = slice(rb * GQA_RB, (rb + 1) * GQA_RB)
            g = rb * GQA_RB // BLOCK
            s = s_ref[rows, :]
            if band:
                qr = slice(rb * GQA_RB % BLOCK, rb * GQA_RB % BLOCK + GQA_RB)
                s = jnp.concatenate(
                    [s[:, :BLOCK] + bias_ref[0, qr, :], s[:, BLOCK:2 * BLOCK],
                     s[:, 2 * BLOCK:3 * BLOCK] + bias_ref[1, qr, :], s[:, 3 * BLOCK:]], axis=1)
            sk = jnp.full((GQA_RB, 1), sink_ref[h * A_GROUP + g] * LOG2E, F32)
            m = jnp.maximum(jnp.max(s, axis=-1, keepdims=True), sk)
            p = jnp.exp2(s - m)
            l = jnp.sum(p, axis=-1, keepdims=True) + jnp.exp2(sk - m)
            il_ref[rows, :] = jnp.broadcast_to(1.0 / l, (GQA_RB, HEAD_DIM))
            p_ref[rows, :] = p.astype(BF16)
        o = jnp.dot(p_ref[...], vcat, preferred_element_type=F32) * il_ref[...]
        for g in range(A_GROUP):
            cs = slice((h * A_GROUP + g) * HEAD_DIM, (h * A_GROUP + g + 1) * HEAD_DIM)
            o_ref[:, cs] = (o[g * BLOCK:(g + 1) * BLOCK] * g_ref[:, cs].astype(F32)).astype(o_ref.dtype)


def _gqa(proj, proj_c, sink, band):
    src = proj if band else proj_c
    bsz, s, _ = src.shape
    l_ctx = proj_c.shape[1]
    nb = s // BLOCK
    qd = A_HEADS * HEAD_DIM
    kvd = A_KV_HEADS * HEAD_DIM
    k_blk, v_blk = 2 * qd // kvd, 2 * qd // kvd + 1
    in_specs = [
        pl.BlockSpec(memory_space=pltpu.SMEM),
        pl.BlockSpec((None, BLOCK, qd), lambda b, i: (b, i, 0)),
        pl.BlockSpec((None, BLOCK, qd), lambda b, i: (b, i, 1)),
    ]
    args = [sink, src, src]
    if band:
        for blk in (k_blk, v_blk):
            in_specs += [
                pl.BlockSpec((None, BLOCK, kvd), lambda b, i, blk=blk: (b, jnp.maximum(i - 1, 0), blk)),
                pl.BlockSpec((None, BLOCK, kvd), lambda b, i, blk=blk: (b, i, blk)),
                pl.BlockSpec((None, BLOCK, kvd), lambda b, i, blk=blk: (b, jnp.minimum(i + 1, nb - 1), blk)),
            ]
            args += [proj, proj, proj]
    in_specs += [
        pl.BlockSpec((None, l_ctx, kvd), lambda b, i: (b, 0, k_blk)),
        pl.BlockSpec((None, l_ctx, kvd), lambda b, i: (b, 0, v_blk)),
    ]
    args += [proj_c, proj_c]
    n_keys = (3 * BLOCK if band else 0) + l_ctx
    rows_all = A_GROUP * BLOCK
    scratch = [pltpu.VMEM((rows_all, n_keys), F32),
               pltpu.VMEM((rows_all, n_keys), BF16),
               pltpu.VMEM((rows_all, HEAD_DIM), F32)]
    if band:
        scratch.append(pltpu.VMEM((2, BLOCK, BLOCK), F32))
    return pl.pallas_call(
        functools.partial(_gqa_body, band=band, n_blocks=nb),
        grid=(bsz, nb),
        in_specs=in_specs,
        out_specs=pl.BlockSpec((None, BLOCK, qd), lambda b, i: (b, i, 0)),
        out_shape=jax.ShapeDtypeStruct((bsz, s, qd), BF16),
        scratch_shapes=scratch,
        compiler_params=_params(16 * 1024 * 1024, 2),
        name="window_gqa" if band else "ctx_gqa",
    )(*args)


SCONV_HALO = 16


def _sconv_body(bg_ref, cg_ref, u_ref, g_ref, cgp_ref, up_ref, cgn_ref, un_ref,
                w_ref, b_ref, o_ref, z_ref, *, n_tiles):
    i = pl.program_id(1)
    tm = bg_ref.shape[0]
    h8 = SCONV_HALO
    z = cg_ref[...].astype(F32) * u_ref[...].astype(F32)
    z_prev = cgp_ref[h8 - 1:h8, :].astype(F32) * up_ref[h8 - 1:h8, :].astype(F32)
    z_next = cgn_ref[0:1, :].astype(F32) * un_ref[0:1, :].astype(F32)
    z_ref[h8:h8 + tm, :] = z
    z_ref[h8 - 1:h8, :] = jnp.where(i > 0, z_prev, 0.0)
    z_ref[h8 + tm:h8 + tm + 1, :] = jnp.where(i < n_tiles - 1, z_next, 0.0)
    conv = (w_ref[0:1, :] * z_ref[h8 - 1:h8 - 1 + tm, :] + w_ref[1:2, :] * z
            + w_ref[2:3, :] * z_ref[h8 + 1:h8 + 1 + tm, :] + b_ref[...])
    o_ref[...] = (bg_ref[...].astype(F32) * conv * g_ref[...].astype(F32)).astype(o_ref.dtype)


def _short_conv(proj, conv_w, conv_b):
    bsz, s, n4 = proj.shape
    d = n4 // 4
    tm = min(512, s)
    tc = 1024
    nt = s // tm
    ncb = d // tc
    h8 = SCONV_HALO
    rb = tm // h8
    last8 = s // h8 - 1

    def main(k):
        return pl.BlockSpec((None, tm, tc), lambda b, i, c, k=k: (b, i, k * ncb + c))

    def halo_prev(k):
        return pl.BlockSpec((None, h8, tc), lambda b, i, c, k=k: (b, jnp.maximum(i * rb - 1, 0), k * ncb + c))

    def halo_next(k):
        return pl.BlockSpec((None, h8, tc), lambda b, i, c, k=k: (b, jnp.minimum((i + 1) * rb, last8), k * ncb + c))

    return pl.pallas_call(
        functools.partial(_sconv_body, n_tiles=nt),
        grid=(bsz, nt, ncb),
        in_specs=[main(0), main(1), main(2), main(3),
                  halo_prev(1), halo_prev(2), halo_next(1), halo_next(2),
                  pl.BlockSpec((SC_WIDTH, tc), lambda b, i, c: (0, c)),
                  pl.BlockSpec((1, tc), lambda b, i, c: (0, c))],
        out_specs=pl.BlockSpec((None, tm, tc), lambda b, i, c: (b, i, c)),
        out_shape=jax.ShapeDtypeStruct((bsz, s, d), BF16),
        scratch_shapes=[pltpu.VMEM((tm + 2 * h8, tc), F32)],
        compiler_params=_params(2 * 5 * tm * tc * 2 + 6 * tm * tc * 4, 3),
        name="short_conv",
    )(proj, proj, proj, proj, proj, proj, proj, proj, conv_w, conv_b.reshape(1, d))


DIFF_TQ = 1024
DIFF_TK = 768
DIFF_RB = 16


def _diff_body(q_ref, k_ref, v_ref, g_ref, hg_ref, lq1_ref, lk1_ref, lq2_ref, lk2_ref,
               o_ref, s0_ref, s1_ref, p0_ref, p1_ref, m_ref, l_ref, al_ref, acc_ref,
               *, lam_init, tk):
    tq = q_ref.shape[0]
    n_chunks = k_ref.shape[0] // tk
    n_lane_groups = tk // V7X_LANES
    nt_dims = (((1,), (1,)), ((), ()))
    s_refs = (s0_ref, s1_ref)
    p_refs = (p0_ref, p1_ref)
    m_ref[...] = jnp.full(m_ref.shape, NEG_BIG, F32)
    l_ref[...] = jnp.zeros(l_ref.shape, F32)
    acc_ref[...] = jnp.zeros(acc_ref.shape, F32)

    def scores(j, c):
        off = pl.multiple_of(c * tk, tk)
        hs = slice(j * HEAD_DIM, (j + 1) * HEAD_DIM)
        s_refs[j][...] = lax.dot_general(q_ref[:, hs], k_ref[pl.ds(off, tk), hs], nt_dims,
                                         preferred_element_type=F32)

    def softmax(j):
        for rb in range(tq // DIFF_RB):
            rows = slice(rb * DIFF_RB, (rb + 1) * DIFF_RB)
            s = s_refs[j][rows, :]
            m_old = m_ref[j, rows, :]
            m_new = jnp.maximum(m_old, jnp.max(s, axis=-1, keepdims=True))
            alpha = jnp.exp2(m_old - m_new)
            p = jnp.exp2(s - jnp.concatenate([m_new] * n_lane_groups, axis=1))
            l_ref[j, rows, :] = alpha * l_ref[j, rows, :] + jnp.sum(p, axis=-1, keepdims=True)
            m_ref[j, rows, :] = m_new
            al_ref[j, rows, :] = alpha
            p_refs[j][rows, :] = p.astype(BF16)

    def weighted_values(j, c):
        off = pl.multiple_of(c * tk, tk)
        al = al_ref[j]
        acc_ref[j] = (jnp.concatenate([al, al], axis=1) * acc_ref[j]
                      + jnp.dot(p_refs[j][...], v_ref[pl.ds(off, tk), :],
                                preferred_element_type=F32))

    def step(c, carry):
        scores(0, c)
        softmax(0)
        scores(1, c)
        weighted_values(0, c)
        softmax(1)
        weighted_values(1, c)
        return carry

    lax.fori_loop(0, n_chunks, step, 0)
    lam = (jnp.exp(jnp.sum(lq1_ref[...] * lk1_ref[...], keepdims=True))
           - jnp.exp(jnp.sum(lq2_ref[...] * lk2_ref[...], keepdims=True)) + lam_init)
    l1 = jnp.concatenate([l_ref[0], l_ref[0]], axis=1)
    l2 = jnp.concatenate([l_ref[1], l_ref[1]], axis=1)
    o = acc_ref[0] / l1 - lam * (acc_ref[1] / l2)
    o = o * lax.rsqrt(jnp.mean(o * o, axis=-1, keepdims=True) + EPS) * hg_ref[...]
    o = o * (1.0 - lam_init)
    o_ref[...] = (o * g_ref[...].astype(F32)).astype(o_ref.dtype)


def _diff_attn(proj, s_lat, head_g, lq1, lk1, lq2, lk2, lam_init):
    bsz, s_all, n4 = proj.shape
    cw = n4 // 4
    dv = 2 * HEAD_DIM
    nh = cw // dv
    tq = DIFF_TQ
    vec = lambda a: a.reshape(1, HEAD_DIM)
    vspec = pl.BlockSpec((1, HEAD_DIM), lambda b, h, t: (0, 0))
    est = 2 * 2 * s_all * dv * 2 + 8 * tq * dv * 4 + 6 * tq * DIFF_TK * 4
    return pl.pallas_call(
        functools.partial(_diff_body, lam_init=lam_init, tk=DIFF_TK),
        grid=(bsz, nh, s_lat // tq),
        in_specs=[
            pl.BlockSpec((None, tq, dv), lambda b, h, t: (b, t, h)),
            pl.BlockSpec((None, s_all, dv), lambda b, h, t: (b, 0, nh + h)),
            pl.BlockSpec((None, s_all, dv), lambda b, h, t: (b, 0, 2 * nh + h)),
            pl.BlockSpec((None, tq, dv), lambda b, h, t: (b, t, 3 * nh + h)),
            pl.BlockSpec((1, dv), lambda b, h, t: (0, 0)),
            vspec, vspec, vspec, vspec,
        ],
        out_specs=pl.BlockSpec((None, tq, dv), lambda b, h, t: (b, t, h)),
        out_shape=jax.ShapeDtypeStruct((bsz, s_lat, cw), BF16),
        scratch_shapes=[
            pltpu.VMEM((tq, DIFF_TK), F32),
            pltpu.VMEM((tq, DIFF_TK), F32),
            pltpu.VMEM((tq, DIFF_TK), BF16),
            pltpu.VMEM((tq, DIFF_TK), BF16),
            pltpu.VMEM((2, tq, V7X_LANES), F32),
            pltpu.VMEM((2, tq, V7X_LANES), F32),
            pltpu.VMEM((2, tq, V7X_LANES), F32),
            pltpu.VMEM((2, tq, dv), F32),
        ],
        compiler_params=_params(est, 3),
        name="diff_attn",
    )(proj, proj, proj, proj, head_g.reshape(1, dv), vec(lq1), vec(lk1), vec(lq2), vec(lk2))


CONF_HALO = 16
CONF_TM = 512
CONF_TC = 512
CONF_RB = 64
CONF_CHAINS = 4


def _conf_conv_body(a_ref, b_ref, ap_ref, bp_ref, an_ref, bn_ref, w_ref, cb_ref, o_ref,
                    u_ref, us_ref, *, n_tiles):
    i = pl.program_id(1)
    tm, tc = a_ref.shape
    hl = CONF_HALO
    half = CONF_WIDTH // 2
    sub = V7X_SUBLANES

    def glu(a, b):
        return a.astype(F32) * jax.nn.sigmoid(b.astype(F32))

    for cc in range(tc // V7X_LANES):
        cs = slice(cc * V7X_LANES, (cc + 1) * V7X_LANES)
        u_ref[cc, hl:hl + tm, :] = glu(a_ref[:, cs], b_ref[:, cs])
        u_ref[cc, 0:hl, :] = jnp.where(i > 0, glu(ap_ref[:, cs], bp_ref[:, cs]), 0.0)
        u_ref[cc, hl + tm:hl + tm + hl, :] = jnp.where(
            i < n_tiles - 1, glu(an_ref[:, cs], bn_ref[:, cs]), 0.0)
    span = us_ref.shape[2]
    for r in range(1, sub):
        us_ref[r - 1] = u_ref[:, r:r + span, :]

    def row_block(rb, carry):
        base = pl.multiple_of(rb * CONF_RB, CONF_RB)
        for cc in range(tc // V7X_LANES):
            cs = slice(cc * V7X_LANES, (cc + 1) * V7X_LANES)
            parts = [None] * CONF_CHAINS
            for k in range(CONF_WIDTH):
                off = hl - half + k
                r, al = off % sub, off - off % sub
                rows = pl.ds(base + al, CONF_RB)
                win = u_ref[cc, rows, :] if r == 0 else us_ref[r - 1, cc, rows, :]
                term = w_ref[k:k + 1, cs] * win
                c = k % CONF_CHAINS
                parts[c] = term if parts[c] is None else parts[c] + term
            while len(parts) > 1:
                parts = [parts[n] + parts[n + 1] for n in range(0, len(parts), 2)]
            o_ref[pl.ds(base, CONF_RB), cs] = parts[0] + cb_ref[:, cs]
        return carry

    lax.fori_loop(0, tm // CONF_RB, row_block, 0)


def _conf_conv(proj, conv_w, conv_b):
    bsz, s, n3 = proj.shape
    d = n3 // 3
    tm, tc, hl = CONF_TM, CONF_TC, CONF_HALO
    nt = s // tm
    ncb = d // tc
    rb = tm // hl
    last = s // hl - 1

    def main(k):
        return pl.BlockSpec((None, tm, tc), lambda b, i, c, k=k: (b, i, k * ncb + c))

    def halo_prev(k):
        return pl.BlockSpec((None, hl, tc), lambda b, i, c, k=k: (b, jnp.maximum(i * rb - 1, 0), k * ncb + c))

    def halo_next(k):
        return pl.BlockSpec((None, hl, tc), lambda b, i, c, k=k: (b, jnp.minimum((i + 1) * rb, last), k * ncb + c))

    return pl.pallas_call(
        functools.partial(_conf_conv_body, n_tiles=nt),
        grid=(bsz, nt, ncb),
        in_specs=[main(0), main(1), halo_prev(0), halo_prev(1), halo_next(0), halo_next(1),
                  pl.BlockSpec((CONF_WIDTH, tc), lambda b, i, c: (0, c)),
                  pl.BlockSpec((1, tc), lambda b, i, c: (0, c))],
        out_specs=pl.BlockSpec((None, tm, tc), lambda b, i, c: (b, i, c)),
        out_shape=jax.ShapeDtypeStruct((bsz, s, d), F32),
        scratch_shapes=[
            pltpu.VMEM((tc // V7X_LANES, tm + 2 * hl, V7X_LANES), F32),
            pltpu.VMEM((V7X_SUBLANES - 1, tc // V7X_LANES, tm + 2 * hl - V7X_SUBLANES, V7X_LANES),
                       F32)],
        compiler_params=_params(8 * (tm + 2 * hl) * tc * 4 + 4 * tm * tc * 2 + 2 * tm * tc * 4, 3),
        name="conf_conv",
    )(proj, proj, proj, proj, proj, proj, conv_w, conv_b.reshape(1, d))


def _ln_gate_body(u_ref, g_ref, lg_ref, lb_ref, o_ref):
    u = u_ref[...]
    uc = u - jnp.mean(u, axis=-1, keepdims=True)
    y = uc * lax.rsqrt(jnp.mean(uc * uc, axis=-1, keepdims=True) + EPS) * lg_ref[...] + lb_ref[...]
    y = y * jax.nn.sigmoid(y)
    o_ref[...] = (y * g_ref[...].astype(F32)).astype(o_ref.dtype)


def _ln_gate(u, proj, ln_g, ln_b):
    bsz, s, d = u.shape
    tm = ROW_TM
    return pl.pallas_call(
        _ln_gate_body,
        grid=(bsz, s // tm),
        in_specs=[pl.BlockSpec((None, tm, d), lambda b, t: (b, t, 0)),
                  pl.BlockSpec((None, tm, d), lambda b, t: (b, t, 2)),
                  pl.BlockSpec((1, d), lambda b, t: (0, 0)),
                  pl.BlockSpec((1, d), lambda b, t: (0, 0))],
        out_specs=pl.BlockSpec((None, tm, d), lambda b, t: (b, t, 0)),
        out_shape=jax.ShapeDtypeStruct((bsz, s, d), BF16),
        compiler_params=_params(2 * tm * d * 8 + 4 * tm * d * 4, 2),
        name="ln_gate",
    )(u, proj, ln_g.reshape(1, d), ln_b.reshape(1, d))


def _rope_tables(n, q_scale):
    pos = jnp.arange(n, dtype=jnp.int32)
    rc = jnp.stack([pos // GRID_W, pos % GRID_W], axis=-1).astype(F32)
    inv_freq = ROPE_BASE ** (-jnp.arange(ROPE_FREQS, dtype=F32) / ROPE_FREQS)
    ang = rc[:, :, None] * inv_freq
    cos, sin = jnp.cos(ang), jnp.sin(ang)
    cs = jnp.concatenate([cos[:, 0], cos[:, 0], cos[:, 1], cos[:, 1]], axis=-1)
    sn = jnp.concatenate([-sin[:, 0], sin[:, 0], -sin[:, 1], sin[:, 1]], axis=-1)
    one, zero = jnp.ones_like(cs), jnp.zeros_like(sn)
    return (jnp.stack([one, cs * q_scale, cs, one * q_scale]),
            jnp.stack([zero, sn * q_scale, sn, zero]))


def _tiles(*parts):
    out, lo = [], 0
    for width, mode in parts:
        hi = lo + width // MM_TN
        out.append((lo, hi, mode))
        lo = hi
    return tuple(out)


def kernel(x, c, ctx, c_ctx, norm_g, ada_w, ada_b, final_g, a_w_in, a_sink, a_w_out, b_w_in, b_conv_w, b_conv_b, b_w_out, c_w_in, c_lam_q1, c_lam_k1, c_lam_q2, c_lam_k2, c_head_g, c_w_out, d_w_in, d_conv_w, d_conv_b, d_ln_g, d_ln_b, d_w_out):
    bsz, s, d = x.shape
    l_ctx = ctx.shape[1]
    q_scale = HEAD_DIM ** -0.5 * LOG2E
    tabs = _rope_tables(s, q_scale)
    tabs_c = (tabs[0][:, :l_ctx], tabs[1][:, :l_ctx])
    lat_row = lambda b: b
    ctx_row = lambda b: 2

    cond8 = jnp.zeros((V7X_SUBLANES, d), F32).at[0:bsz].set(c).at[bsz].set(c_ctx)
    mods = _ada_mods(cond8, ada_w, ada_b).reshape(DEPTH, V7X_SUBLANES, 1, 3 * d)

    qd, kvd = A_HEADS * HEAD_DIM, A_KV_HEADS * HEAD_DIM
    hctx = ctx

    w_in = a_w_in[0]
    w_in = jnp.concatenate([w_in[:, :qd], w_in[:, qd + 2 * kvd:], w_in[:, qd:qd + 2 * kvd]],
                           axis=1).astype(BF16)
    w_out = a_w_out[0].astype(BF16)
    h = _norm_mod(x, norm_g[0], mods[0], lat_row)
    hc = _norm_mod(hctx, norm_g[0], mods[0], ctx_row)
    proj = _proj(h, w_in, tabs,
                 _tiles((qd, "rope_q"), (qd, "silu"), (kvd, "rope_k"), (kvd, "plain")))
    proj_c = _proj(hc, w_in, tabs_c,
                   _tiles((qd, "scale"), (qd, "silu"), (2 * kvd, "plain")))
    y = _gqa(proj, proj_c, a_sink[0], band=True)
    y_c = _gqa(proj, proj_c, a_sink[0], band=False)
    x = _out_proj(y, w_out, x, mods[0], lat_row)
    hctx = _out_proj(y_c, w_out, hctx, mods[0], ctx_row)

    w_in = b_w_in[0].astype(BF16)
    w_out = b_w_out[0].astype(BF16)
    modes = _tiles((3 * d, "plain"), (d, "silu"))
    h = _norm_mod(x, norm_g[1], mods[1], lat_row)
    hc = _norm_mod(hctx, norm_g[1], mods[1], ctx_row)
    y = _short_conv(_proj(h, w_in, tabs, modes), b_conv_w[0], b_conv_b[0])
    y_c = _short_conv(_proj(hc, w_in, tabs_c, modes), b_conv_w[0], b_conv_b[0])
    x = _out_proj(y, w_out, x, mods[1], lat_row)
    hctx = _out_proj(y_c, w_out, hctx, mods[1], ctx_row)

    w_in = c_w_in[0].astype(BF16)
    w_out = c_w_out[0].astype(BF16)
    cw = w_in.shape[1] // 4
    lam_init = 0.8 - 0.6 * math.exp(-0.3 * 2)
    h = _norm_mod(x, norm_g[2], mods[2], lat_row)
    hc = _norm_mod(hctx, norm_g[2], mods[2], ctx_row)
    proj = _proj(h, w_in, tabs,
                 _tiles((cw, "rope_q"), (cw, "rope_k"), (cw, "plain"), (cw, "silu")),
                 out_rows=s + l_ctx)
    proj = _proj(hc, w_in, tabs_c, _tiles((4 * cw, "plain")),
                 col_tiles=(cw // MM_TN, 2 * cw // MM_TN), out=proj, out_rows=s + l_ctx,
                 out_row_block=s // l_ctx)
    y = _diff_attn(proj, s, c_head_g[0], c_lam_q1[0], c_lam_k1[0], c_lam_q2[0], c_lam_k2[0],
                   lam_init)
    x = _out_proj(y, w_out, x, mods[2], lat_row)

    w_in = d_w_in[0].astype(BF16)
    w_out = d_w_out[0].astype(BF16)
    h = _norm_mod(x, norm_g[3], mods[3], lat_row)
    proj = _proj(h, w_in, tabs, _tiles((2 * d, "plain"), (d, "silu")))
    u = _conf_conv(proj, d_conv_w[0], d_conv_b[0])
    y = _ln_gate(u, proj, d_ln_g[0], d_ln_b[0])
    x = _out_proj(y, w_out, x, mods[3], lat_row)

    return _final_norm(x, final_g)
```

```python
import functools
import math

import jax
import jax.numpy as jnp
from jax import lax
from jax.experimental import pallas as pl
from jax.experimental.pallas import tpu as pltpu

F32 = jnp.float32
BF16 = jnp.bfloat16

D_MODEL = 4096
DEPTH = 4
GRID_W = 64
HEAD_DIM = 128
A_HEADS = 32
A_KV_HEADS = 8
A_GROUP = 4
BLOCK = 128
SC_WIDTH = 3
C_HEADS = 16
CONF_WIDTH = 31
ROPE_FREQS = 32
ROPE_BASE = 10000.0
EPS = 1e-6
LOG2E = math.log2(math.e)
NEG_BIG = -1e30

V7X_VMEM_BYTES = 64 * 1024 * 1024
V7X_LANES = 128
V7X_SUBLANES = 8
VMEM_CEILING = V7X_VMEM_BYTES - 8 * 1024 * 1024

MM_TM = 1024
MM_TN = 1024
OUT_TM = 1024
ROW_TM = 512


def _params(est_bytes, n_grid):
    limit = int(min(VMEM_CEILING, max(32 * 1024 * 1024, est_bytes * 5 // 4)))
    return pltpu.CompilerParams(
        dimension_semantics=("arbitrary",) * n_grid, vmem_limit_bytes=limit)


def _ada_body(s_ref, w_ref, b_ref, o_ref):
    s = s_ref[...]
    s = s * jax.nn.sigmoid(s)
    o_ref[...] = jnp.dot(s.astype(BF16), w_ref[...].astype(BF16),
                         preferred_element_type=F32) + b_ref[...]


def _ada_mods(cond8, ada_w, ada_b):
    depth, d, n = ada_w.shape
    tn = 512
    return pl.pallas_call(
        _ada_body,
        grid=(depth, n // tn),
        in_specs=[
            pl.BlockSpec((V7X_SUBLANES, d), lambda l, j: (0, 0)),
            pl.BlockSpec((None, d, tn), lambda l, j: (l, 0, j)),
            pl.BlockSpec((None, 1, tn), lambda l, j: (l, 0, j)),
        ],
        out_specs=pl.BlockSpec((None, V7X_SUBLANES, tn), lambda l, j: (l, 0, j)),
        out_shape=jax.ShapeDtypeStruct((depth, V7X_SUBLANES, n), F32),
        compiler_params=_params(2 * d * tn * 4 + d * tn * 2, 2),
        name="ada_mods",
    )(cond8, ada_w, ada_b.reshape(depth, 1, n))


NORM_RB = 16
NORM_UNROLL = 4
NORM_CHAINS = 4


def _norm_mod_body(x_ref, g_ref, shift_ref, scale_ref, o_ref, gs_ref):
    tm, d = x_ref.shape
    lanes = [slice(c * V7X_LANES, (c + 1) * V7X_LANES) for c in range(d // V7X_LANES)]
    gs_ref[...] = g_ref[...] * (1.0 + scale_ref[...])

    def row_blocks(it, carry):
        for u in range(NORM_UNROLL):
            rows = pl.ds(pl.multiple_of((it * NORM_UNROLL + u) * NORM_RB, NORM_RB), NORM_RB)
            parts = [None] * NORM_CHAINS
            for n, ls in enumerate(lanes):
                xl = x_ref[rows, ls]
                c = n % NORM_CHAINS
                parts[c] = xl * xl if parts[c] is None else parts[c] + xl * xl
            while len(parts) > 1:
                parts = [parts[n] + parts[n + 1] for n in range(0, len(parts), 2)]
            r = lax.rsqrt(jnp.sum(parts[0], axis=-1, keepdims=True) * (1.0 / d) + EPS)
            for ls in lanes:
                y = x_ref[rows, ls] * r * gs_ref[:, ls] + shift_ref[:, ls]
                o_ref[rows, ls] = y.astype(o_ref.dtype)
        return carry

    lax.fori_loop(0, tm // (NORM_RB * NORM_UNROLL), row_blocks, 0)


def _norm_mod(x, g, mods, mod_row):
    bsz, s, d = x.shape
    tm = min(ROW_TM, s)
    return pl.pallas_call(
        _norm_mod_body,
        grid=(bsz, s // tm),
        in_specs=[
            pl.BlockSpec((None, tm, d), lambda b, t: (b, t, 0)),
            pl.BlockSpec((1, d), lambda b, t: (0, 0)),
            pl.BlockSpec((None, 1, d), lambda b, t: (mod_row(b), 0, 0)),
            pl.BlockSpec((None, 1, d), lambda b, t: (mod_row(b), 0, 1)),
        ],
        out_specs=pl.BlockSpec((None, tm, d), lambda b, t: (b, t, 0)),
        out_shape=jax.ShapeDtypeStruct((bsz, s, d), BF16),
        scratch_shapes=[pltpu.VMEM((1, d), F32)],
        compiler_params=_params(2 * tm * d * 6, 2),
        name="norm_mod",
    )(x, g.reshape(1, d), mods, mods)


def _final_norm_body(x_ref, g_ref, o_ref):
    x = x_ref[...]
    o_ref[...] = x * lax.rsqrt(jnp.mean(x * x, axis=-1, keepdims=True) + EPS) * g_ref[...]


def _final_norm(x, g):
    bsz, s, d = x.shape
    tm = ROW_TM
    return pl.pallas_call(
        _final_norm_body,
        grid=(bsz, s // tm),
        in_specs=[pl.BlockSpec((None, tm, d), lambda b, t: (b, t, 0)),
                  pl.BlockSpec((1, d), lambda b, t: (0, 0))],
        out_specs=pl.BlockSpec((None, tm, d), lambda b, t: (b, t, 0)),
        out_shape=jax.ShapeDtypeStruct((bsz, s, d), F32),
        compiler_params=_params(2 * tm * d * 8 + 2 * tm * d * 4, 2),
        name="final_norm",
    )(x, g.reshape(1, d))


TAB_IDENTITY, TAB_ROPE_Q, TAB_ROPE_K, TAB_SCALE = 0, 1, 2, 3
_MODE_TABLE = {"plain": TAB_IDENTITY, "silu": TAB_IDENTITY, "rope_q": TAB_ROPE_Q,
               "rope_k": TAB_ROPE_K, "scale": TAB_SCALE}


def _proj_body(h_ref, w_ref, cs_ref, sn_ref, o_ref, *, silu_range, rotate):
    acc = jnp.dot(h_ref[...], w_ref[...], preferred_element_type=F32)
    tm, tn = acc.shape
    if rotate:
        cs = cs_ref[...]
        sn = sn_ref[...]
        lane = lax.broadcasted_iota(jnp.int32, (tm, HEAD_DIM), 1)
        first_half = (lane % (2 * ROPE_FREQS)) < ROPE_FREQS
    if silu_range is not None:
        j = pl.program_id(2)
        is_silu = (j >= silu_range[0]) & (j < silu_range[1])
    for hh in range(tn // HEAD_DIM):
        sl = slice(hh * HEAD_DIM, (hh + 1) * HEAD_DIM)
        y = acc[:, sl]
        if rotate:
            swapped = jnp.where(first_half, pltpu.roll(y, 96, 1), pltpu.roll(y, 32, 1))
            y = y * cs + swapped * sn
        if silu_range is not None:
            y = jnp.where(is_silu, y * jax.nn.sigmoid(y), y)
        o_ref[:, sl] = y.astype(o_ref.dtype)


def _proj(h, w, tabs, modes):
    bsz, s, d = h.shape
    n = w.shape[1]
    tm = min(MM_TM, s)
    tn = MM_TN
    rotate = any(_MODE_TABLE[m] != TAB_IDENTITY for _, _, m in modes)
    silu = [(lo, hi) for lo, hi, m in modes if m == "silu"]
    assert len(silu) <= 1
    body = functools.partial(_proj_body, silu_range=silu[0] if silu else None, rotate=rotate)

    def table_of(j):
        kind = jnp.int32(TAB_IDENTITY)
        for lo, hi, m in modes:
            if _MODE_TABLE[m] != TAB_IDENTITY:
                kind = jnp.where((j >= lo) & (j < hi), _MODE_TABLE[m], kind)
        return kind

    est = 2 * (tm * d * 2 + d * tn * 2 + tm * tn * 2) + 2 * tm * tn * 4
    return pl.pallas_call(
        body,
        grid=(bsz, s // tm, n // tn),
        in_specs=[
            pl.BlockSpec((None, tm, d), lambda b, i, j: (b, i, 0)),
            pl.BlockSpec((d, tn), lambda b, i, j: (0, j)),
            pl.BlockSpec((None, tm, HEAD_DIM), lambda b, i, j: (table_of(j), i, 0)),
            pl.BlockSpec((None, tm, HEAD_DIM), lambda b, i, j: (table_of(j), i, 0)),
        ],
        out_specs=pl.BlockSpec((None, tm, tn), lambda b, i, j: (b, i, j)),
        out_shape=jax.ShapeDtypeStruct((bsz, s, n), BF16),
        compiler_params=_params(est, 3),
        name="in_proj",
    )(h, w, tabs[0], tabs[1])


def _out_body(y_ref, w_ref, x_ref, gate_ref, o_ref):
    acc = jnp.dot(y_ref[...], w_ref[...], preferred_element_type=F32)
    o_ref[...] = x_ref[...] + gate_ref[...] * acc


def _out_proj(y, w, x, mods, mod_row):
    bsz, s, d = y.shape
    n = w.shape[1]
    tm = min(OUT_TM, s)
    tn = MM_TN
    gate_blk = 2 * (D_MODEL // tn)
    est = 2 * (tm * d * 2 + d * tn * 2 + 2 * tm * tn * 4) + tm * tn * 4
    return pl.pallas_call(
        _out_body,
        grid=(bsz, s // tm, n // tn),
        in_specs=[
            pl.BlockSpec((None, tm, d), lambda b, i, j: (b, i, 0)),
            pl.BlockSpec((d, tn), lambda b, i, j: (0, j)),
            pl.BlockSpec((None, tm, tn), lambda b, i, j: (b, i, j)),
            pl.BlockSpec((None, 1, tn), lambda b, i, j: (mod_row(b), 0, gate_blk + j)),
        ],
        out_specs=pl.BlockSpec((None, tm, tn), lambda b, i, j: (b, i, j)),
        out_shape=jax.ShapeDtypeStruct((bsz, s, n), F32),
        compiler_params=_params(est, 3),
        name="out_proj",
    )(y, w, x, mods)


GQA_RB = 16


def _gqa_body(sink_ref, q_ref, g_ref, *rest, band, n_blocks):
    if band:
        (kp_ref, ko_ref, kn_ref, vp_ref, vo_ref, vn_ref, kc_ref, vc_ref, o_ref,
         s_ref, p_ref, il_ref, bias_ref) = rest
    else:
        kc_ref, vc_ref, o_ref, s_ref, p_ref, il_ref = rest
    i = pl.program_id(1)
    rows_all = A_GROUP * BLOCK
    nt_dims = (((1,), (1,)), ((), ()))
    if band:
        r = lax.broadcasted_iota(jnp.int32, (BLOCK, BLOCK), 0)
        c = lax.broadcasted_iota(jnp.int32, (BLOCK, BLOCK), 1)
        bias_ref[0] = jnp.where((c >= r) & (i > 0), 0.0, NEG_BIG)
        bias_ref[1] = jnp.where((c <= r) & (i < n_blocks - 1), 0.0, NEG_BIG)
    for h in range(A_KV_HEADS):
        hs = slice(h * HEAD_DIM, (h + 1) * HEAD_DIM)
        q4 = jnp.concatenate(
            [q_ref[:, (h * A_GROUP + g) * HEAD_DIM:(h * A_GROUP + g + 1) * HEAD_DIM]
             for g in range(A_GROUP)], axis=0)
        if band:
            kcat = jnp.concatenate([kp_ref[:, hs], ko_ref[:, hs], kn_ref[:, hs], kc_ref[:, hs]], axis=0)
            vcat = jnp.concatenate([vp_ref[:, hs], vo_ref[:, hs], vn_ref[:, hs], vc_ref[:, hs]], axis=0)
        else:
            kcat = kc_ref[:, hs]
            vcat = vc_ref[:, hs]
        s_ref[...] = lax.dot_general(q4, kcat, nt_dims, preferred_element_type=F32)
        for rb in range(rows_all // GQA_RB):
            rows = slice(rb * GQA_RB, (rb + 1) * GQA_RB)
            g = rb * GQA_RB // BLOCK
            s = s_ref[rows, :]
            if band:
                qr = slice(rb * GQA_RB % BLOCK, rb * GQA_RB % BLOCK + GQA_RB)
                s = jnp.concatenate(
                    [s[:, :BLOCK] + bias_ref[0, qr, :], s[:, BLOCK:2 * BLOCK],
                     s[:, 2 * BLOCK:3 * BLOCK] + bias_ref[1, qr, :], s[:, 3 * BLOCK:]], axis=1)
            sk = jnp.full((GQA_RB, 1), sink_ref[h * A_GROUP + g] * LOG2E, F32)
            m = jnp.maximum(jnp.max(s, axis=-1, keepdims=True), sk)
            p = jnp.exp2(s - m)
            l = jnp.sum(p, axis=-1, keepdims=True) + jnp.exp2(sk - m)
            il_ref[rows, :] = jnp.broadcast_to(1.0 / l, (GQA_RB, HEAD_DIM))
            p_ref[rows, :] = p.astype(BF16)
        o = jnp.dot(p_ref[...], vcat, preferred_element_type=F32) * il_ref[...]
        for g in range(A_GROUP):
            cs = slice((h * A_GROUP + g) * HEAD_DIM, (h * A_GROUP + g + 1) * HEAD_DIM)
            o_ref[:, cs] = (o[g * BLOCK:(g + 1) * BLOCK] * g_ref[:, cs].astype(F32)).astype(o_ref.dtype)


def _gqa(proj, proj_c, sink, band):
    src = proj if band else proj_c
    bsz, s, _ = src.shape
    l_ctx = proj_c.shape[1]
    nb = s // BLOCK
    qd = A_HEADS * HEAD_DIM
    kvd = A_KV_HEADS * HEAD_DIM
    k_blk, v_blk = 2 * qd // kvd, 2 * qd // kvd + 1
    in_specs = [
        pl.BlockSpec(memory_space=pltpu.SMEM),
        pl.BlockSpec((None, BLOCK, qd), lambda b, i: (b, i, 0)),
        pl.BlockSpec((None, BLOCK, qd), lambda b, i: (b, i, 1)),
    ]
    args = [sink, src, src]
    if band:
        for blk in (k_blk, v_blk):
            in_specs += [
                pl.BlockSpec((None, BLOCK, kvd), lambda b, i, blk=blk: (b, jnp.maximum(i - 1, 0), blk)),
                pl.BlockSpec((None, BLOCK, kvd), lambda b, i, blk=blk: (b, i, blk)),
                pl.BlockSpec((None, BLOCK, kvd), lambda b, i, blk=blk: (b, jnp.minimum(i + 1, nb - 1), blk)),
            ]
            args += [proj, proj, proj]
    in_specs += [
        pl.BlockSpec((None, l_ctx, kvd), lambda b, i: (b, 0, k_blk)),
        pl.BlockSpec((None, l_ctx, kvd), lambda b, i: (b, 0, v_blk)),
    ]
    args += [proj_c, proj_c]
    n_keys = (3 * BLOCK if band else 0) + l_ctx
    rows_all = A_GROUP * BLOCK
    scratch = [pltpu.VMEM((rows_all, n_keys), F32),
               pltpu.VMEM((rows_all, n_keys), BF16),
               pltpu.VMEM((rows_all, HEAD_DIM), F32)]
    if band:
        scratch.append(pltpu.VMEM((2, BLOCK, BLOCK), F32))
    return pl.pallas_call(
        functools.partial(_gqa_body, band=band, n_blocks=nb),
        grid=(bsz, nb),
        in_specs=in_specs,
        out_specs=pl.BlockSpec((None, BLOCK, qd), lambda b, i: (b, i, 0)),
        out_shape=jax.ShapeDtypeStruct((bsz, s, qd), BF16),
        scratch_shapes=scratch,
        compiler_params=_params(16 * 1024 * 1024, 2),
        name="window_gqa" if band else "ctx_gqa",
    )(*args)


SCONV_HALO = 16


def _sconv_body(bg_ref, cg_ref, u_ref, g_ref, cgp_ref, up_ref, cgn_ref, un_ref,
                w_ref, b_ref, o_ref, z_ref, *, n_tiles):
    i = pl.program_id(1)
    tm = bg_ref.shape[0]
    h8 = SCONV_HALO
    z = cg_ref[...].astype(F32) * u_ref[...].astype(F32)
    z_prev = cgp_ref[h8 - 1:h8, :].astype(F32) * up_ref[h8 - 1:h8, :].astype(F32)
    z_next = cgn_ref[0:1, :].astype(F32) * un_ref[0:1, :].astype(F32)
    z_ref[h8:h8 + tm, :] = z
    z_ref[h8 - 1:h8, :] = jnp.where(i > 0, z_prev, 0.0)
    z_ref[h8 + tm:h8 + tm + 1, :] = jnp.where(i < n_tiles - 1, z_next, 0.0)
    conv = (w_ref[0:1, :] * z_ref[h8 - 1:h8 - 1 + tm, :] + w_ref[1:2, :] * z
            + w_ref[2:3, :] * z_ref[h8 + 1:h8 + 1 + tm, :] + b_ref[...])
    o_ref[...] = (bg_ref[...].astype(F32) * conv * g_ref[...].astype(F32)).astype(o_ref.dtype)


def _short_conv(proj, conv_w, conv_b):
    bsz, s, n4 = proj.shape
    d = n4 // 4
    tm = min(512, s)
    tc = 1024
    nt = s // tm
    ncb = d // tc
    h8 = SCONV_HALO
    rb = tm // h8
    last8 = s // h8 - 1

    def main(k):
        return pl.BlockSpec((None, tm, tc), lambda b, i, c, k=k: (b, i, k * ncb + c))

    def halo_prev(k):
        return pl.BlockSpec((None, h8, tc), lambda b, i, c, k=k: (b, jnp.maximum(i * rb - 1, 0), k * ncb + c))

    def halo_next(k):
        return pl.BlockSpec((None, h8, tc), lambda b, i, c, k=k: (b, jnp.minimum((i + 1) * rb, last8), k * ncb + c))

    return pl.pallas_call(
        functools.partial(_sconv_body, n_tiles=nt),
        grid=(bsz, nt, ncb),
        in_specs=[main(0), main(1), main(2), main(3),
                  halo_prev(1), halo_prev(2), halo_next(1), halo_next(2),
                  pl.BlockSpec((SC_WIDTH, tc), lambda b, i, c: (0, c)),
                  pl.BlockSpec((1, tc), lambda b, i, c: (0, c))],
        out_specs=pl.BlockSpec((None, tm, tc), lambda b, i, c: (b, i, c)),
        out_shape=jax.ShapeDtypeStruct((bsz, s, d), BF16),
        scratch_shapes=[pltpu.VMEM((tm + 2 * h8, tc), F32)],
        compiler_params=_params(2 * 5 * tm * tc * 2 + 6 * tm * tc * 4, 3),
        name="short_conv",
    )(proj, proj, proj, proj, proj, proj, proj, proj, conv_w, conv_b.reshape(1, d))


DIFF_TQ = 1024
DIFF_TK = 768
DIFF_RB = 16


def _diff_body(q_ref, k_ref, v_ref, kc_ref, vc_ref, g_ref, hg_ref, lq1_ref, lk1_ref, lq2_ref,
               lk2_ref, o_ref, kall_ref, vall_ref, s0_ref, s1_ref, p0_ref, p1_ref, m_ref, l_ref,
               al_ref, acc_ref, *, lam_init, tk):
    tq = q_ref.shape[0]
    s_lat = k_ref.shape[0]
    n_chunks = kall_ref.shape[0] // tk
    n_lane_groups = tk // V7X_LANES
    nt_dims = (((1,), (1,)), ((), ()))
    s_refs = (s0_ref, s1_ref)
    p_refs = (p0_ref, p1_ref)

    @pl.when(pl.program_id(2) == 0)
    def _():
        kall_ref[0:s_lat, :] = k_ref[...]
        kall_ref[s_lat:, :] = kc_ref[...]
        vall_ref[0:s_lat, :] = v_ref[...]
        vall_ref[s_lat:, :] = vc_ref[...]

    m_ref[...] = jnp.full(m_ref.shape, NEG_BIG, F32)
    l_ref[...] = jnp.zeros(l_ref.shape, F32)
    acc_ref[...] = jnp.zeros(acc_ref.shape, F32)

    def scores(j, c):
        off = pl.multiple_of(c * tk, tk)
        hs = slice(j * HEAD_DIM, (j + 1) * HEAD_DIM)
        s_refs[j][...] = lax.dot_general(q_ref[:, hs], kall_ref[pl.ds(off, tk), hs], nt_dims,
                                         preferred_element_type=F32)

    def softmax(j):
        for rb in range(tq // DIFF_RB):
            rows = slice(rb * DIFF_RB, (rb + 1) * DIFF_RB)
            s = s_refs[j][rows, :]
            m_old = m_ref[j, rows, :]
            m_new = jnp.maximum(m_old, jnp.max(s, axis=-1, keepdims=True))
            alpha = jnp.exp2(m_old - m_new)
            p = jnp.exp2(s - jnp.concatenate([m_new] * n_lane_groups, axis=1))
            l_ref[j, rows, :] = alpha * l_ref[j, rows, :] + jnp.sum(p, axis=-1, keepdims=True)
            m_ref[j, rows, :] = m_new
            al_ref[j, rows, :] = alpha
            p_refs[j][rows, :] = p.astype(BF16)

    def weighted_values(j, c):
        off = pl.multiple_of(c * tk, tk)
        al = al_ref[j]
        acc_ref[j] = (jnp.concatenate([al, al], axis=1) * acc_ref[j]
                      + jnp.dot(p_refs[j][...], vall_ref[pl.ds(off, tk), :],
                                preferred_element_type=F32))

    def step(c, carry):
        scores(0, c)
        softmax(0)
        scores(1, c)
        weighted_values(0, c)
        softmax(1)
        weighted_values(1, c)
        return carry

    lax.fori_loop(0, n_chunks, step, 0)
    lam = (jnp.exp(jnp.sum(lq1_ref[...] * lk1_ref[...], keepdims=True))
           - jnp.exp(jnp.sum(lq2_ref[...] * lk2_ref[...], keepdims=True)) + lam_init)
    l1 = jnp.concatenate([l_ref[0], l_ref[0]], axis=1)
    l2 = jnp.concatenate([l_ref[1], l_ref[1]], axis=1)
    o = acc_ref[0] / l1 - lam * (acc_ref[1] / l2)
    o = o * lax.rsqrt(jnp.mean(o * o, axis=-1, keepdims=True) + EPS) * hg_ref[...]
    o = o * (1.0 - lam_init)
    o_ref[...] = (o * g_ref[...].astype(F32)).astype(o_ref.dtype)


def _diff_attn(proj, proj_c, head_g, lq1, lk1, lq2, lk2, lam_init):
    bsz, s_lat, n4 = proj.shape
    l_ctx = proj_c.shape[1]
    cw = n4 // 4
    dv = 2 * HEAD_DIM
    nh = cw // dv
    tq, tk = DIFF_TQ, DIFF_TK
    vec = lambda a: a.reshape(1, HEAD_DIM)
    vspec = pl.BlockSpec((1, HEAD_DIM), lambda b, h, t: (0, 0))
    est = (3 * 2 * (s_lat + l_ctx) * dv * 2 + 2 * tq * tk * 6 + 3 * 2 * tq * V7X_LANES * 4
           + 2 * tq * dv * 4 + 3 * 2 * tq * dv * 2)
    return pl.pallas_call(
        functools.partial(_diff_body, lam_init=lam_init, tk=tk),
        grid=(bsz, nh, s_lat // tq),
        in_specs=[
            pl.BlockSpec((None, tq, dv), lambda b, h, t: (b, t, h)),
            pl.BlockSpec((None, s_lat, dv), lambda b, h, t: (b, 0, nh + h)),
            pl.BlockSpec((None, s_lat, dv), lambda b, h, t: (b, 0, 2 * nh + h)),
            pl.BlockSpec((None, l_ctx, dv), lambda b, h, t: (b, 0, h)),
            pl.BlockSpec((None, l_ctx, dv), lambda b, h, t: (b, 0, nh + h)),
            pl.BlockSpec((None, tq, dv), lambda b, h, t: (b, t, 3 * nh + h)),
            pl.BlockSpec((1, dv), lambda b, h, t: (0, 0)),
            vspec, vspec, vspec, vspec,
        ],
        out_specs=pl.BlockSpec((None, tq, dv), lambda b, h, t: (b, t, h)),
        out_shape=jax.ShapeDtypeStruct((bsz, s_lat, cw), BF16),
        scratch_shapes=[
            pltpu.VMEM((s_lat + l_ctx, dv), BF16),
            pltpu.VMEM((s_lat + l_ctx, dv), BF16),
            pltpu.VMEM((tq, tk), F32),
            pltpu.VMEM((tq, tk), F32),
            pltpu.VMEM((tq, tk), BF16),
            pltpu.VMEM((tq, tk), BF16),
            pltpu.VMEM((2, tq, V7X_LANES), F32),
            pltpu.VMEM((2, tq, V7X_LANES), F32),
            pltpu.VMEM((2, tq, V7X_LANES), F32),
            pltpu.VMEM((2, tq, dv), F32),
        ],
        compiler_params=_params(est, 3),
        name="diff_attn",
    )(proj, proj, proj, proj_c, proj_c, proj, head_g.reshape(1, dv),
      vec(lq1), vec(lk1), vec(lq2), vec(lk2))


CONF_HALO = 16
CONF_TM = 512
CONF_TC = 512
CONF_RB = 64
CONF_CHAINS = 4


def _conf_conv_body(a_ref, b_ref, ap_ref, bp_ref, an_ref, bn_ref, w_ref, cb_ref, o_ref,
                    u_ref, us_ref, *, n_tiles):
    i = pl.program_id(1)
    tm, tc = a_ref.shape
    hl = CONF_HALO
    half = CONF_WIDTH // 2
    sub = V7X_SUBLANES

    def glu(a, b):
        return a.astype(F32) * jax.nn.sigmoid(b.astype(F32))

    for cc in range(tc // V7X_LANES):
        cs = slice(cc * V7X_LANES, (cc + 1) * V7X_LANES)
        u_ref[cc, hl:hl + tm, :] = glu(a_ref[:, cs], b_ref[:, cs])
        u_ref[cc, 0:hl, :] = jnp.where(i > 0, glu(ap_ref[:, cs], bp_ref[:, cs]), 0.0)
        u_ref[cc, hl + tm:hl + tm + hl, :] = jnp.where(
            i < n_tiles - 1, glu(an_ref[:, cs], bn_ref[:, cs]), 0.0)
    span = us_ref.shape[2]
    for r in range(1, sub):
        us_ref[r - 1] = u_ref[:, r:r + span, :]

    def row_block(rb, carry):
        base = pl.multiple_of(rb * CONF_RB, CONF_RB)
        for cc in range(tc // V7X_LANES):
            cs = slice(cc * V7X_LANES, (cc + 1) * V7X_LANES)
            parts = [None] * CONF_CHAINS
            for k in range(CONF_WIDTH):
                off = hl - half + k
                r, al = off % sub, off - off % sub
                rows = pl.ds(base + al, CONF_RB)
                win = u_ref[cc, rows, :] if r == 0 else us_ref[r - 1, cc, rows, :]
                term = w_ref[k:k + 1, cs] * win
                c = k % CONF_CHAINS
                parts[c] = term if parts[c] is None else parts[c] + term
            while len(parts) > 1:
                parts = [parts[n] + parts[n + 1] for n in range(0, len(parts), 2)]
            o_ref[pl.ds(base, CONF_RB), cs] = parts[0] + cb_ref[:, cs]
        return carry

    lax.fori_loop(0, tm // CONF_RB, row_block, 0)


def _conf_conv(proj, conv_w, conv_b):
    bsz, s, n3 = proj.shape
    d = n3 // 3
    tm, tc, hl = CONF_TM, CONF_TC, CONF_HALO
    nt = s // tm
    ncb = d // tc
    rb = tm // hl
    last = s // hl - 1

    def main(k):
        return pl.BlockSpec((None, tm, tc), lambda b, i, c, k=k: (b, i, k * ncb + c))

    def halo_prev(k):
        return pl.BlockSpec((None, hl, tc), lambda b, i, c, k=k: (b, jnp.maximum(i * rb - 1, 0), k * ncb + c))

    def halo_next(k):
        return pl.BlockSpec((None, hl, tc), lambda b, i, c, k=k: (b, jnp.minimum((i + 1) * rb, last), k * ncb + c))

    return pl.pallas_call(
        functools.partial(_conf_conv_body, n_tiles=nt),
        grid=(bsz, nt, ncb),
        in_specs=[main(0), main(1), halo_prev(0), halo_prev(1), halo_next(0), halo_next(1),
                  pl.BlockSpec((CONF_WIDTH, tc), lambda b, i, c: (0, c)),
                  pl.BlockSpec((1, tc), lambda b, i, c: (0, c))],
        out_specs=pl.BlockSpec((None, tm, tc), lambda b, i, c: (b, i, c)),
        out_shape=jax.ShapeDtypeStruct((bsz, s, d), F32),
        scratch_shapes=[
            pltpu.VMEM((tc // V7X_LANES, tm + 2 * hl, V7X_LANES), F32),
            pltpu.VMEM((V7X_SUBLANES - 1, tc // V7X_LANES, tm + 2 * hl - V7X_SUBLANES, V7X_LANES),
                       F32)],
        compiler_params=_params(8 * (tm + 2 * hl) * tc * 4 + 4 * tm * tc * 2 + 2 * tm * tc * 4, 3),
        name="conf_conv",
    )(proj, proj, proj, proj, proj, proj, conv_w, conv_b.reshape(1, d))


def _ln_gate_body(u_ref, g_ref, lg_ref, lb_ref, o_ref):
    u = u_ref[...]
    uc = u - jnp.mean(u, axis=-1, keepdims=True)
    y = uc * lax.rsqrt(jnp.mean(uc * uc, axis=-1, keepdims=True) + EPS) * lg_ref[...] + lb_ref[...]
    y = y * jax.nn.sigmoid(y)
    o_ref[...] = (y * g_ref[...].astype(F32)).astype(o_ref.dtype)


def _ln_gate(u, proj, ln_g, ln_b):
    bsz, s, d = u.shape
    tm = ROW_TM
    return pl.pallas_call(
        _ln_gate_body,
        grid=(bsz, s // tm),
        in_specs=[pl.BlockSpec((None, tm, d), lambda b, t: (b, t, 0)),
                  pl.BlockSpec((None, tm, d), lambda b, t: (b, t, 2)),
                  pl.BlockSpec((1, d), lambda b, t: (0, 0)),
                  pl.BlockSpec((1, d), lambda b, t: (0, 0))],
        out_specs=pl.BlockSpec((None, tm, d), lambda b, t: (b, t, 0)),
        out_shape=jax.ShapeDtypeStruct((bsz, s, d), BF16),
        compiler_params=_params(2 * tm * d * 8 + 4 * tm * d * 4, 2),
        name="ln_gate",
    )(u, proj, ln_g.reshape(1, d), ln_b.reshape(1, d))


def _rope_tables(n, q_scale):
    pos = jnp.arange(n, dtype=jnp.int32)
    rc = jnp.stack([pos // GRID_W, pos % GRID_W], axis=-1).astype(F32)
    inv_freq = ROPE_BASE ** (-jnp.arange(ROPE_FREQS, dtype=F32) / ROPE_FREQS)
    ang = rc[:, :, None] * inv_freq
    cos, sin = jnp.cos(ang), jnp.sin(ang)
    cs = jnp.concatenate([cos[:, 0], cos[:, 0], cos[:, 1], cos[:, 1]], axis=-1)
    sn = jnp.concatenate([-sin[:, 0], sin[:, 0], -sin[:, 1], sin[:, 1]], axis=-1)
    one, zero = jnp.ones_like(cs), jnp.zeros_like(sn)
    return (jnp.stack([one, cs * q_scale, cs, one * q_scale]),
            jnp.stack([zero, sn * q_scale, sn, zero]))


def _tiles(*parts):
    out, lo = [], 0
    for width, mode in parts:
        hi = lo + width // MM_TN
        out.append((lo, hi, mode))
        lo = hi
    return tuple(out)


def kernel(x, c, ctx, c_ctx, norm_g, ada_w, ada_b, final_g, a_w_in, a_sink, a_w_out, b_w_in, b_conv_w, b_conv_b, b_w_out, c_w_in, c_lam_q1, c_lam_k1, c_lam_q2, c_lam_k2, c_head_g, c_w_out, d_w_in, d_conv_w, d_conv_b, d_ln_g, d_ln_b, d_w_out):
    bsz, s, d = x.shape
    l_ctx = ctx.shape[1]
    q_scale = HEAD_DIM ** -0.5 * LOG2E
    tabs = _rope_tables(s, q_scale)
    tabs_c = (tabs[0][:, :l_ctx], tabs[1][:, :l_ctx])
    lat_row = lambda b: b
    ctx_row = lambda b: 2

    cond8 = jnp.zeros((V7X_SUBLANES, d), F32).at[0:bsz].set(c).at[bsz].set(c_ctx)
    mods = _ada_mods(cond8, ada_w, ada_b).reshape(DEPTH, V7X_SUBLANES, 1, 3 * d)

    qd, kvd = A_HEADS * HEAD_DIM, A_KV_HEADS * HEAD_DIM
    hctx = ctx

    w_in = a_w_in[0]
    w_in = jnp.concatenate([w_in[:, :qd], w_in[:, qd + 2 * kvd:], w_in[:, qd:qd + 2 * kvd]],
                           axis=1).astype(BF16)
    w_out = a_w_out[0].astype(BF16)
    h = _norm_mod(x, norm_g[0], mods[0], lat_row)
    hc = _norm_mod(hctx, norm_g[0], mods[0], ctx_row)
    proj = _proj(h, w_in, tabs,
                 _tiles((qd, "rope_q"), (qd, "silu"), (kvd, "rope_k"), (kvd, "plain")))
    proj_c = _proj(hc, w_in, tabs_c,
                   _tiles((qd, "scale"), (qd, "silu"), (2 * kvd, "plain")))
    y = _gqa(proj, proj_c, a_sink[0], band=True)
    y_c = _gqa(proj, proj_c, a_sink[0], band=False)
    x = _out_proj(y, w_out, x, mods[0], lat_row)
    hctx = _out_proj(y_c, w_out, hctx, mods[0], ctx_row)

    w_in = b_w_in[0].astype(BF16)
    w_out = b_w_out[0].astype(BF16)
    modes = _tiles((3 * d, "plain"), (d, "silu"))
    h = _norm_mod(x, norm_g[1], mods[1], lat_row)
    hc = _norm_mod(hctx, norm_g[1], mods[1], ctx_row)
    y = _short_conv(_proj(h, w_in, tabs, modes), b_conv_w[0], b_conv_b[0])
    y_c = _short_conv(_proj(hc, w_in, tabs_c, modes), b_conv_w[0], b_conv_b[0])
    x = _out_proj(y, w_out, x, mods[1], lat_row)
    hctx = _out_proj(y_c, w_out, hctx, mods[1], ctx_row)

    w_in = c_w_in[0].astype(BF16)
    w_out = c_w_out[0].astype(BF16)
    cw = w_in.shape[1] // 4
    lam_init = 0.8 - 0.6 * math.exp(-0.3 * 2)
    h = _norm_mod(x, norm_g[2], mods[2], lat_row)
    hc = _norm_mod(hctx, norm_g[2], mods[2], ctx_row)
    proj = _proj(h, w_in, tabs,
                 _tiles((cw, "rope_q"), (cw, "rope_k"), (cw, "plain"), (cw, "silu")))
    proj_c = _proj(hc, w_in[:, cw:3 * cw], tabs_c, _tiles((2 * cw, "plain")))
    y = _diff_attn(proj, proj_c, c_head_g[0], c_lam_q1[0], c_lam_k1[0], c_lam_q2[0],
                   c_lam_k2[0], lam_init)
    x = _out_proj(y, w_out, x, mods[2], lat_row)

    w_in = d_w_in[0].astype(BF16)
    w_out = d_w_out[0].astype(BF16)
    h = _norm_mod(x, norm_g[3], mods[3], lat_row)
    proj = _proj(h, w_in, tabs, _tiles((2 * d, "plain"), (d, "silu")))
    u = _conf_conv(proj, d_conv_w[0], d_conv_b[0])
    y = _ln_gate(u, proj, d_ln_g[0], d_ln_b[0])
    x = _out_proj(y, w_out, x, mods[3], lat_row)

    return _final_norm(x, final_g)
```
